```python
import jax, jax.numpy as jnp
from jax import lax
import numpy as np

D_MODEL = 4096
BATCH = 1
SEQ = 8192
DEPTH = 2

N_MIXERS = 2
N_RET_LAYERS = (DEPTH + 1) // N_MIXERS
N_FNO_LAYERS = DEPTH // N_MIXERS
RET_HEADS = 16
RET_QK_DIM = D_MODEL // RET_HEADS
RET_V_DIM = 2 * RET_QK_DIM
RET_V_WIDTH = RET_HEADS * RET_V_DIM
RET_IN_WIDTH = 2 * D_MODEL + 2 * RET_V_WIDTH
RET_CHUNK = 128
ROPE_BASE = 10000.0
FNO_GROUPS = 8
FNO_GROUP_DIM = D_MODEL // FNO_GROUPS
D_FF = 4 * D_MODEL
N_MOD = 6
EPS = 1e-6

kernel_name = "hybrid_retention_fourier_adaln_encoder"


def rms_norm(x, g):
    xf = x.astype(jnp.float32)
    y = xf * lax.rsqrt(jnp.mean(xf * xf, axis=-1, keepdims=True) + EPS)
    return (y * g.astype(jnp.float32)).astype(x.dtype)


def adaln_params(c, w, b):
    m = jax.nn.silu(c) @ w + b
    return jnp.split(m[:, None, :], N_MOD, axis=-1)


def rotary(x, pos):
    half = x.shape[-1] // 2
    inv = ROPE_BASE ** (-jnp.arange(half, dtype=jnp.float32) / half)
    ang = pos[:, None] * inv[None, :]
    cos = jnp.cos(ang)[None, :, None, :]
    sin = jnp.sin(ang)[None, :, None, :]
    x1, x2 = x[..., :half], x[..., half:]
    return jnp.concatenate([x1 * cos - x2 * sin, x1 * sin + x2 * cos], axis=-1)


def retention_chunkwise(q, k, v, log_gamma):
    B, S, H, dk = q.shape
    dv = v.shape[-1]
    C = RET_CHUNK
    n = S // C
    qc = q.reshape(B, n, C, H, dk)
    kc = k.reshape(B, n, C, H, dk)
    vc = v.reshape(B, n, C, H, dv)
    j = jnp.arange(C, dtype=jnp.float32)
    diff = j[:, None] - j[None, :]
    intra = jnp.where(diff[None] >= 0,
                      jnp.exp(jnp.maximum(diff, 0.0)[None] * log_gamma[:, None, None]),
                      0.0)
    scores = jnp.einsum('bnihd,bnjhd->bnhij', qc, kc) * intra[None, None]
    out_intra = jnp.einsum('bnhij,bnjhe->bnihe', scores, vc)
    q_decay = jnp.exp((j + 1.0)[:, None] * log_gamma[None, :])
    k_decay = jnp.exp((C - 1.0 - j)[:, None] * log_gamma[None, :])
    chunk_decay = jnp.exp(C * log_gamma)

    def step(state, xs):
        qi, ki, vi = xs
        out = jnp.einsum('bihd,bhde->bihe', qi * q_decay[None, :, :, None], state)
        state = state * chunk_decay[None, :, None, None] + jnp.einsum(
            'bihd,bihe->bhde', ki * k_decay[None, :, :, None], vi)
        return state, out

    state0 = jnp.zeros((B, H, dk, dv), jnp.float32)
    xs = (qc.transpose(1, 0, 2, 3, 4), kc.transpose(1, 0, 2, 3, 4), vc.transpose(1, 0, 2, 3, 4))
    _, out_cross = lax.scan(step, state0, xs)
    out = out_intra + out_cross.transpose(1, 0, 2, 3, 4)
    return out.reshape(B, S, H, dv)


def retention_mixer(h, w_in, w_out, gn_g, dec_fwd, dec_bwd):
    B, S, _ = h.shape
    proj = h @ w_in
    q, k, v, g = jnp.split(proj, [D_MODEL, 2 * D_MODEL, 2 * D_MODEL + RET_V_WIDTH], axis=-1)
    pos = jnp.arange(S, dtype=jnp.float32)
    q = rotary(q.astype(jnp.float32).reshape(B, S, RET_HEADS, RET_QK_DIM), pos)
    k = rotary(k.astype(jnp.float32).reshape(B, S, RET_HEADS, RET_QK_DIM), pos) * (RET_QK_DIM ** -0.5)
    v = v.astype(jnp.float32).reshape(B, S, RET_HEADS, RET_V_DIM)
    lg_f = jax.nn.log_sigmoid(dec_fwd.astype(jnp.float32))
    lg_b = jax.nn.log_sigmoid(dec_bwd.astype(jnp.float32))
    y_f = retention_chunkwise(q, k, v, lg_f)
    y_b = jnp.flip(retention_chunkwise(jnp.flip(q, 1), jnp.flip(k, 1), jnp.flip(v, 1), lg_b), 1)
    y = y_f + y_b
    mu = jnp.mean(y, axis=-1, keepdims=True)
    var = jnp.mean(jnp.square(y - mu), axis=-1, keepdims=True)
    y = ((y - mu) * lax.rsqrt(var + EPS)).reshape(B, S, RET_V_WIDTH) * gn_g.astype(jnp.float32)
    y = jax.nn.silu(g.astype(jnp.float32)) * y
    return y.astype(h.dtype) @ w_out


def fourier_mixer(h, w_f, b_f):
    B, S, D = h.shape
    hg = h.astype(jnp.float32).reshape(B, S, FNO_GROUPS, FNO_GROUP_DIM)
    mixed = jnp.fft.fft2(hg, axes=(1, 3)).real.reshape(B, S, D)
    return mixed.astype(h.dtype) @ w_f + b_f


def sq_relu_mlp(h, w1, w2):
    a = jax.nn.relu(h @ w1)
    return (a * a) @ w2


def setup_inputs(seed: int = 0) -> dict:
    key = jax.random.key(seed)
    ks = jax.random.split(key, 16)
    f32 = jnp.float32
    D = D_MODEL
    base_logit = np.log(2.0 ** (5.0 + np.arange(RET_HEADS)) - 1.0).astype(np.float32)
    return {
        "x": jax.random.normal(ks[0], (BATCH, SEQ, D), f32),
        "c": jax.random.normal(ks[1], (BATCH, D), f32),
        "ada_w": jax.random.normal(ks[2], (DEPTH, D, N_MOD * D), f32) * (0.5 * D ** -0.5),
        "ada_b": jax.random.normal(ks[3], (DEPTH, N_MOD * D), f32) * 0.02,
        "norm_g": 1.0 + 0.02 * jax.random.normal(ks[4], (DEPTH, 4, D), f32),
        "ret_w_in": jax.random.normal(ks[5], (N_RET_LAYERS, D, RET_IN_WIDTH), f32) * D ** -0.5,
        "ret_w_out": jax.random.normal(ks[6], (N_RET_LAYERS, RET_V_WIDTH, D), f32) * RET_V_WIDTH ** -0.5,
        "ret_gn_g": 1.0 + 0.02 * jax.random.normal(ks[7], (N_RET_LAYERS, RET_V_WIDTH), f32),
        "ret_decay_fwd": jnp.asarray(base_logit)[None, :] + 0.1 * jax.random.normal(ks[8], (N_RET_LAYERS, RET_HEADS), f32),
        "ret_decay_bwd": jnp.asarray(base_logit[::-1].copy())[None, :] + 0.1 * jax.random.normal(ks[9], (N_RET_LAYERS, RET_HEADS), f32),
        "fno_w": jax.random.normal(ks[10], (N_FNO_LAYERS, D, D), f32) * D ** -0.5,
        "fno_b": jax.random.normal(ks[11], (N_FNO_LAYERS, D), f32) * 0.02,
        "mlp_w1": jax.random.normal(ks[12], (DEPTH, D, D_FF), f32) * D ** -0.5,
        "mlp_w2": jax.random.normal(ks[13], (DEPTH, D_FF, D), f32) * D_FF ** -0.5,
    }


def reference(x, c, ada_w, ada_b, norm_g, ret_w_in, ret_w_out, ret_gn_g,
              ret_decay_fwd, ret_decay_bwd, fno_w, fno_b, mlp_w1, mlp_w2):
    for layer in range(DEPTH):
        occ = layer // N_MIXERS
        sh1, sc1, g1, sh2, sc2, g2 = adaln_params(c, ada_w[layer], ada_b[layer])
        h = rms_norm(x, norm_g[layer, 0]) * (1.0 + sc1) + sh1
        if layer % N_MIXERS == 0:
            y = retention_mixer(h, ret_w_in[occ], ret_w_out[occ], ret_gn_g[occ],
                                ret_decay_fwd[occ], ret_decay_bwd[occ])
        else:
            y = fourier_mixer(h, fno_w[occ], fno_b[occ])
        x = x + g1 * rms_norm(y, norm_g[layer, 1])
        h = rms_norm(x, norm_g[layer, 2]) * (1.0 + sc2) + sh2
        x = x + g2 * rms_norm(sq_relu_mlp(h, mlp_w1[layer], mlp_w2[layer]), norm_g[layer, 3])
    return x
```

```python
import functools

import numpy as np
import jax
import jax.numpy as jnp
from jax import lax
from jax.experimental import pallas as pl
from jax.experimental.pallas import tpu as pltpu

N_MOD = 6
RET_HEADS = 16
FNO_GROUPS = 8
ROPE_BASE = 10000.0
EPS = 1e-6

RET_CHUNK = 256
RET_BLOCK = 1024
DFT_N1 = 64
VMEM_LIMIT_BYTES = 56 * 1024 * 1024

F32 = jnp.float32
BF16 = jnp.bfloat16


def _params(*sem):
    return pltpu.CompilerParams(dimension_semantics=sem, vmem_limit_bytes=VMEM_LIMIT_BYTES)


def _adaln_kernel(c_ref, w_ref, b_ref, o_ref):
    c = c_ref[...]
    s = c * jax.nn.sigmoid(c)
    o_ref[0] = jnp.sum(w_ref[0] * s, axis=0, keepdims=True) + b_ref[0]


def _adaln(c_col, ada_w, ada_b):
    depth, d, n = ada_w.shape
    bn = min(512, n)
    return pl.pallas_call(
        _adaln_kernel,
        grid=(depth, n // bn),
        in_specs=[pl.BlockSpec((d, 1), lambda l, j: (0, 0)),
                  pl.BlockSpec((1, d, bn), lambda l, j: (l, 0, j)),
                  pl.BlockSpec((1, 1, bn), lambda l, j: (l, 0, j))],
        out_specs=pl.BlockSpec((1, 1, bn), lambda l, j: (l, 0, j)),
        out_shape=jax.ShapeDtypeStruct((depth, 1, n), F32),
        compiler_params=_params("parallel", "parallel"),
        name="adaln",
    )(c_col, ada_w, ada_b.reshape(depth, 1, n))


def _rms(x, g):
    ms = jnp.mean(x * x, axis=-1, keepdims=True)
    return x * lax.rsqrt(ms + EPS) * g


def _row_kernel(*refs, has_y, has_h):
    refs = list(refs)
    x_ref = refs.pop(0)
    x = x_ref[...]
    if has_y:
        y_ref, gate_ref, gy_ref = refs.pop(0), refs.pop(0), refs.pop(0)
    if has_h:
        gx_ref, sc_ref, sh_ref = refs.pop(0), refs.pop(0), refs.pop(0)
    if has_y:
        xo_ref = refs.pop(0)
        x = x + gate_ref[...] * _rms(y_ref[...].astype(F32), gy_ref[...])
        xo_ref[...] = x
    if has_h:
        h_ref = refs.pop(0)
        h_ref[...] = (_rms(x, gx_ref[...]) * (1.0 + sc_ref[...]) + sh_ref[...]).astype(BF16)


def _row_call(x, y=None, gate=None, gy=None, gx=None, sc=None, sh=None):
    s, d = x.shape
    bm = min(256, s)
    has_y, has_h = y is not None, gx is not None
    row = pl.BlockSpec((bm, d), lambda i: (i, 0))
    vec = pl.BlockSpec((1, d), lambda i: (0, 0))
    args, in_specs, out_specs, out_shape = [x], [row], [], []
    if has_y:
        args += [y, gate, gy]
        in_specs += [row, vec, vec]
        out_specs.append(row)
        out_shape.append(jax.ShapeDtypeStruct((s, d), F32))
    if has_h:
        args += [gx, sc, sh]
        in_specs += [vec, vec, vec]
        out_specs.append(row)
        out_shape.append(jax.ShapeDtypeStruct((s, d), BF16))
    out = pl.pallas_call(
        functools.partial(_row_kernel, has_y=has_y, has_h=has_h),
        grid=(s // bm,),
        in_specs=in_specs, out_specs=out_specs, out_shape=out_shape,
        compiler_params=_params("parallel"),
        name="row_norm",
    )(*args)
    return out


def _mm_kernel(*refs, nk, act, has_bias):
    if has_bias:
        a_ref, w_ref, b_ref, o_ref, acc_ref = refs
    else:
        a_ref, w_ref, o_ref, acc_ref = refs
        b_ref = None
    k = pl.program_id(2)
    part = jnp.dot(a_ref[...], w_ref[...], preferred_element_type=F32)

    def finish(acc):
        if has_bias:
            acc = acc + b_ref[...]
        if act == "relu2":
            acc = jnp.maximum(acc, 0.0)
            acc = acc * acc
        o_ref[...] = acc.astype(o_ref.dtype)

    if nk == 1:
        finish(part)
    else:
        @pl.when(k == 0)
        def _():
            acc_ref[...] = part

        @pl.when(jnp.logical_and(k > 0, k < nk - 1))
        def _():
            acc_ref[...] += part

        @pl.when(k == nk - 1)
        def _():
            finish(acc_ref[...] + part)


def _matmul(a, w, bias=None, act=None, out_dtype=F32, bm=1024, bn=1024, bk=2048):
    m, kd = a.shape
    _, n = w.shape
    bm, bn, bk = min(bm, m), min(bn, n), min(bk, kd)
    nk = kd // bk
    has_bias = bias is not None
    in_specs = [pl.BlockSpec((bm, bk), lambda i, j, k: (i, k)),
                pl.BlockSpec((bk, bn), lambda i, j, k: (k, j))]
    args = [a, w]
    if has_bias:
        in_specs.append(pl.BlockSpec((1, bn), lambda i, j, k: (0, j)))
        args.append(bias)
    return pl.pallas_call(
        functools.partial(_mm_kernel, nk=nk, act=act, has_bias=has_bias),
        grid=(m // bm, n // bn, nk),
        in_specs=in_specs,
        out_specs=pl.BlockSpec((bm, bn), lambda i, j, k: (i, j)),
        out_shape=jax.ShapeDtypeStruct((m, n), out_dtype),
        scratch_shapes=[pltpu.VMEM((bm, bn), F32)],
        compiler_params=_params("parallel", "parallel", "arbitrary"),
        name="matmul",
    )(*args)


def _rotate(x, cos, sin):
    half = x.shape[-1] // 2
    x1, x2 = x[:, :half], x[:, half:]
    return jnp.concatenate([x1 * cos - x2 * sin, x1 * sin + x2 * cos], axis=-1)


def _ret_kernel(lg_ref, q_ref, k_ref, v_ref, g_ref, cos_ref, sin_ref, gn_ref,
                o_ref, yacc_ref, st_ref, *, chunk, nsub, k_scale):
    h, p, i = pl.program_id(0), pl.program_id(1), pl.program_id(2)
    nblk = pl.num_programs(2)
    blk_rows = chunk * nsub
    lgf, lgb = lg_ref[0, h], lg_ref[1, h]

    @pl.when(i == 0)
    def _():
        st_ref[...] = jnp.zeros_like(st_ref)

    r = lax.broadcasted_iota(jnp.int32, (chunk, 1), 0).astype(F32)

    def rotated(ref, rows, scale):
        x = ref[rows, :].astype(F32)
        x = _rotate(x, cos_ref[rows, :], sin_ref[rows, :])
        return x if scale == 1.0 else x * scale

    def state_update(kr, kdec, v, lg):
        kd = (kr * kdec).astype(BF16)
        upd = lax.dot_general(kd, v, (((0,), (0,)), ((), ())), preferred_element_type=F32)
        st_ref[...] = st_ref[...] * jnp.exp(jnp.full((1, 1), chunk * lg, F32)) + upd

    @pl.when(p == 0)
    def _():
        ri = lax.broadcasted_iota(jnp.int32, (chunk, chunk), 0)
        ci = lax.broadcasted_iota(jnp.int32, (chunk, chunk), 1)
        diff = (ri - ci).astype(F32)
        dmat = (jnp.where(diff >= 0, jnp.exp(jnp.maximum(diff, 0.0) * lgf), 0.0)
                + jnp.where(diff <= 0, jnp.exp(jnp.maximum(-diff, 0.0) * lgb), 0.0))
        qdec = jnp.exp((r + 1.0) * lgf)
        kdec = jnp.exp((chunk - 1.0 - r) * lgf)
        for s in range(nsub):
            rows = pl.ds(s * chunk, chunk)
            qr = rotated(q_ref, rows, 1.0)
            kr = rotated(k_ref, rows, k_scale)
            v = v_ref[rows, :]
            sc = lax.dot_general(qr.astype(BF16), kr.astype(BF16), (((1,), (1,)), ((), ())),
                                 preferred_element_type=F32)
            intra = jnp.dot((sc * dmat).astype(BF16), v, preferred_element_type=F32)
            cross = jnp.dot((qr * qdec).astype(BF16), st_ref[...].astype(BF16),
                            preferred_element_type=F32)
            yacc_ref[pl.ds(pl.multiple_of(i * blk_rows + s * chunk, chunk), chunk), :] = intra + cross
            state_update(kr, kdec, v, lgf)

    @pl.when(p == 1)
    def _():
        qdec = jnp.exp((chunk - r) * lgb)
        kdec = jnp.exp(r * lgb)
        base = (nblk - 1 - i) * blk_rows
        for s in reversed(range(nsub)):
            rows = pl.ds(s * chunk, chunk)
            qr = rotated(q_ref, rows, 1.0)
            kr = rotated(k_ref, rows, k_scale)
            v = v_ref[rows, :]
            cross = jnp.dot((qr * qdec).astype(BF16), st_ref[...].astype(BF16),
                            preferred_element_type=F32)
            y = yacc_ref[pl.ds(pl.multiple_of(base + s * chunk, chunk), chunk), :] + cross
            mu = jnp.mean(y, axis=-1, keepdims=True)
            yc = y - mu
            var = jnp.mean(yc * yc, axis=-1, keepdims=True)
            yn = yc * lax.rsqrt(var + EPS) * gn_ref[...]
            g = g_ref[rows, :].astype(F32)
            o_ref[rows, :] = (g * jax.nn.sigmoid(g) * yn).astype(o_ref.dtype)
            state_update(kr, kdec, v, lgb)


def _retention(proj, lg, cos, sin, gn_g, heads):
    s, width = proj.shape
    d = width // 6
    dk, dv = d // heads, 2 * d // heads
    chunk = min(RET_CHUNK, s)
    blk = min(RET_BLOCK, s)
    nblk = s // blk

    def cidx(p, i):
        return i + p * (nblk - 1 - 2 * i)

    grid_spec = pltpu.PrefetchScalarGridSpec(
        num_scalar_prefetch=1,
        grid=(heads, 2, nblk),
        in_specs=[
            pl.BlockSpec((blk, dk), lambda h, p, i, lg: (cidx(p, i), h)),
            pl.BlockSpec((blk, dk), lambda h, p, i, lg: (cidx(p, i), heads + h)),
            pl.BlockSpec((blk, dv), lambda h, p, i, lg: (cidx(p, i), heads + h)),
            pl.BlockSpec((blk, dv), lambda h, p, i, lg: (nblk - 1 - i * p, 2 * heads + h)),
            pl.BlockSpec((blk, dk // 2), lambda h, p, i, lg: (cidx(p, i), 0)),
            pl.BlockSpec((blk, dk // 2), lambda h, p, i, lg: (cidx(p, i), 0)),
            pl.BlockSpec((1, dv), lambda h, p, i, lg: (0, h)),
        ],
        out_specs=pl.BlockSpec((blk, dv), lambda h, p, i, lg: (nblk - 1 - i * p, h)),
        scratch_shapes=[pltpu.VMEM((s, dv), F32), pltpu.VMEM((dk, dv), F32)],
    )
    return pl.pallas_call(
        functools.partial(_ret_kernel, chunk=chunk, nsub=blk // chunk, k_scale=float(dk) ** -0.5),
        grid_spec=grid_spec,
        out_shape=jax.ShapeDtypeStruct((s, 2 * d), BF16),
        compiler_params=_params("arbitrary", "arbitrary", "arbitrary"),
        name="retention",
    )(lg, proj, proj, proj, proj, cos, sin, gn_g)


def _dft_tables(n):
    j = np.arange(n)
    ang = 2.0 * np.pi * ((j[:, None] * j[None, :]) % n) / n
    return np.cos(ang), np.sin(ang)


def _feat_dft_kernel(h_ref, cs_ref, o_ref):
    dg = h_ref.shape[-1]
    r = jnp.dot(h_ref[...], cs_ref[...], preferred_element_type=F32)
    o_ref[0] = r[:, :dg].astype(o_ref.dtype)
    o_ref[1] = r[:, dg:].astype(o_ref.dtype)


def _seq_stage1_kernel(ab_ref, ma_ref, mb_ref, wr_ref, wi_ref, o_ref):
    n1 = ab_ref.shape[1]
    t = (jnp.dot(ma_ref[...], ab_ref[0], preferred_element_type=F32)
         + jnp.dot(mb_ref[...], ab_ref[1], preferred_element_type=F32))
    tr, ti = t[:n1], t[n1:]
    wr, wi = wr_ref[...], wi_ref[...]
    o_ref[0] = (tr * wr - ti * wi).astype(o_ref.dtype)
    o_ref[1] = (tr * wi + ti * wr).astype(o_ref.dtype)


def _seq_stage2_kernel(t_ref, c_ref, s_ref, o_ref):
    o_ref[...] = (jnp.dot(c_ref[...], t_ref[0], preferred_element_type=F32)
                  + jnp.dot(s_ref[...], t_ref[1], preferred_element_type=F32)).astype(o_ref.dtype)


def _fourier_real(h, groups):
    s, d = h.shape
    dg = d // groups
    n1 = min(DFT_N1, s // 8)
    n2 = s // n1

    cd, sd = _dft_tables(dg)
    cs = jnp.asarray(np.concatenate([cd, sd], axis=1), F32).astype(BF16)
    bm = min(1024, s)
    ab = pl.pallas_call(
        _feat_dft_kernel,
        grid=(s // bm, groups),
        in_specs=[pl.BlockSpec((bm, dg), lambda i, g: (i, g)),
                  pl.BlockSpec((dg, 2 * dg), lambda i, g: (0, 0))],
        out_specs=pl.BlockSpec((2, bm, dg), lambda i, g: (0, i, g)),
        out_shape=jax.ShapeDtypeStruct((2, s, d), BF16),
        compiler_params=_params("parallel", "parallel"),
        name="feat_dft",
    )(h, cs)

    c1, s1 = _dft_tables(n1)
    ma = jnp.asarray(np.concatenate([c1, -s1], axis=0), F32).astype(BF16)
    mb = jnp.asarray(np.concatenate([-s1, -c1], axis=0), F32).astype(BF16)
    tw = 2.0 * np.pi * (np.arange(n2)[:, None] * np.arange(n1)[None, :]) / s
    wr = jnp.asarray(np.cos(tw)[:, :, None], F32)
    wi = jnp.asarray(-np.sin(tw)[:, :, None], F32)
    tp = pl.pallas_call(
        _seq_stage1_kernel,
        grid=(n2,),
        in_specs=[pl.BlockSpec((2, n1, d), lambda j: (0, 0, j)),
                  pl.BlockSpec((2 * n1, n1), lambda j: (0, 0)),
                  pl.BlockSpec((2 * n1, n1), lambda j: (0, 0)),
                  pl.BlockSpec((None, n1, 1), lambda j: (j, 0, 0)),
                  pl.BlockSpec((None, n1, 1), lambda j: (j, 0, 0))],
        out_specs=pl.BlockSpec((2, None, n1, d), lambda j: (0, j, 0, 0)),
        out_shape=jax.ShapeDtypeStruct((2, n2, n1, d), BF16),
        compiler_params=_params("parallel"),
        name="seq_dft_stage1",
    )(ab.reshape(2, n1, n2 * d), ma, mb, wr, wi)

    c2, s2 = _dft_tables(n2)
    mixed = pl.pallas_call(
        _seq_stage2_kernel,
        grid=(n1,),
        in_specs=[pl.BlockSpec((2, n2, d), lambda i: (0, 0, i)),
                  pl.BlockSpec((n2, n2), lambda i: (0, 0)),
                  pl.BlockSpec((n2, n2), lambda i: (0, 0))],
        out_specs=pl.BlockSpec((n2, d), lambda i: (0, i)),
        out_shape=jax.ShapeDtypeStruct((n2, n1 * d), BF16),
        compiler_params=_params("parallel"),
        name="seq_dft_stage2",
    )(tp.reshape(2, n2, n1 * d), jnp.asarray(c2, F32).astype(BF16), jnp.asarray(s2, F32).astype(BF16))
    return mixed.reshape(s, d)


def kernel(x, c, ada_w, ada_b, norm_g, ret_w_in, ret_w_out, ret_gn_g, ret_decay_fwd,
           ret_decay_bwd, fno_w, fno_b, mlp_w1, mlp_w2):
    batch, s, d = x.shape
    assert batch == 1
    depth = ada_w.shape[0]
    heads, groups = RET_HEADS, FNO_GROUPS
    x = x.reshape(s, d)

    mods = _adaln(c.reshape(d, 1), ada_w, ada_b)

    def mod(layer, idx):
        return mods[layer, :, idx * d:(idx + 1) * d]

    def gain(layer, idx):
        return norm_g[layer, idx].reshape(1, d)

    half = d // heads // 2
    inv = ROPE_BASE ** (-jnp.arange(half, dtype=F32) / half)
    ang = jnp.arange(s, dtype=F32)[:, None] * inv[None, :]
    cos, sin = jnp.cos(ang), jnp.sin(ang)

    (h,) = _row_call(x, gx=gain(0, 0), sc=mod(0, 1), sh=mod(0, 0))
    for layer in range(depth):
        occ = layer // 2
        if layer % 2 == 0:
            proj = _matmul(h, ret_w_in[occ].astype(BF16), out_dtype=BF16)
            lg = jnp.stack([jax.nn.log_sigmoid(ret_decay_fwd[occ].astype(F32)),
                            jax.nn.log_sigmoid(ret_decay_bwd[occ].astype(F32))])
            yh = _retention(proj, lg, cos, sin, ret_gn_g[occ].reshape(1, 2 * d), heads)
            y = _matmul(yh, ret_w_out[occ].astype(BF16))
        else:
            mixed = _fourier_real(h, groups)
            y = _matmul(mixed, fno_w[occ].astype(BF16), bias=fno_b[occ].reshape(1, d))
        x, h = _row_call(x, y=y, gate=mod(layer, 2), gy=gain(layer, 1),
                         gx=gain(layer, 2), sc=mod(layer, 4), sh=mod(layer, 3))
        a = _matmul(h, mlp_w1[layer].astype(BF16), act="relu2", out_dtype=BF16)
        y = _matmul(a, mlp_w2[layer].astype(BF16))
        if layer + 1 < depth:
            x, h = _row_call(x, y=y, gate=mod(layer, 5), gy=gain(layer, 3),
                             gx=gain(layer + 1, 0), sc=mod(layer + 1, 1), sh=mod(layer + 1, 0))
        else:
            (x,) = _row_call(x, y=y, gate=mod(layer, 5), gy=gain(layer, 3))
    return x.reshape(batch, s, d)
```

```python
import functools

import numpy as np
import jax
import jax.numpy as jnp
from jax import lax
from jax.experimental import pallas as pl
from jax.experimental.pallas import tpu as pltpu

N_MOD = 6
RET_HEADS = 16
FNO_GROUPS = 8
ROPE_BASE = 10000.0
EPS = 1e-6

RET_CHUNK = 256
RET_BLOCK = 1024
DFT_N1 = 64
VMEM_LIMIT_BYTES = 56 * 1024 * 1024

F32 = jnp.float32
BF16 = jnp.bfloat16


def _params(*sem):
    return pltpu.CompilerParams(dimension_semantics=sem, vmem_limit_bytes=VMEM_LIMIT_BYTES)


def _adaln_kernel(c_ref, w_ref, b_ref, o_ref):
    c = c_ref[...]
    s = c * jax.nn.sigmoid(c)
    o_ref[0] = jnp.sum(w_ref[0] * s, axis=0, keepdims=True) + b_ref[0]


def _adaln(c_col, ada_w, ada_b):
    depth, d, n = ada_w.shape
    bn = min(512, n)
    return pl.pallas_call(
        _adaln_kernel,
        grid=(depth, n // bn),
        in_specs=[pl.BlockSpec((d, 1), lambda l, j: (0, 0)),
                  pl.BlockSpec((1, d, bn), lambda l, j: (l, 0, j)),
                  pl.BlockSpec((1, 1, bn), lambda l, j: (l, 0, j))],
        out_specs=pl.BlockSpec((1, 1, bn), lambda l, j: (l, 0, j)),
        out_shape=jax.ShapeDtypeStruct((depth, 1, n), F32),
        compiler_params=_params("parallel", "parallel"),
        name="adaln",
    )(c_col, ada_w, ada_b.reshape(depth, 1, n))


def _rms(x, g):
    ms = jnp.mean(x * x, axis=-1, keepdims=True)
    return x * lax.rsqrt(ms + EPS) * g


def _row_kernel(*refs, has_y, has_h):
    refs = list(refs)
    x_ref = refs.pop(0)
    x = x_ref[...]
    if has_y:
        y_ref, gate_ref, gy_ref = refs.pop(0), refs.pop(0), refs.pop(0)
    if has_h:
        gx_ref, sc_ref, sh_ref = refs.pop(0), refs.pop(0), refs.pop(0)
    if has_y:
        xo_ref = refs.pop(0)
        x = x + gate_ref[...] * _rms(y_ref[...].astype(F32), gy_ref[...])
        xo_ref[...] = x
    if has_h:
        h_ref = refs.pop(0)
        h_ref[...] = (_rms(x, gx_ref[...]) * (1.0 + sc_ref[...]) + sh_ref[...]).astype(BF16)


def _row_call(x, y=None, gate=None, gy=None, gx=None, sc=None, sh=None):
    s, d = x.shape
    bm = min(256, s)
    has_y, has_h = y is not None, gx is not None
    row = pl.BlockSpec((bm, d), lambda i: (i, 0))
    vec = pl.BlockSpec((1, d), lambda i: (0, 0))
    args, in_specs, out_specs, out_shape = [x], [row], [], []
    if has_y:
        args += [y, gate, gy]
        in_specs += [row, vec, vec]
        out_specs.append(row)
        out_shape.append(jax.ShapeDtypeStruct((s, d), F32))
    if has_h:
        args += [gx, sc, sh]
        in_specs += [vec, vec, vec]
        out_specs.append(row)
        out_shape.append(jax.ShapeDtypeStruct((s, d), BF16))
    out = pl.pallas_call(
        functools.partial(_row_kernel, has_y=has_y, has_h=has_h),
        grid=(s // bm,),
        in_specs=in_specs, out_specs=out_specs, out_shape=out_shape,
        compiler_params=_params("parallel"),
        name="row_norm",
    )(*args)
    return out


def _mm_kernel(*refs, nk, bm, act, has_bias):
    refs = list(refs)
    a_ref, w_ref = refs.pop(0), refs.pop(0)
    b_ref = refs.pop(0) if has_bias else None
    o_ref, wb_ref = refs.pop(0), refs.pop(0)
    acc_ref = refs.pop(0) if nk > 1 else None
    k, m = pl.program_id(1), pl.program_id(2)

    @pl.when(m == 0)
    def _():
        wb_ref[...] = w_ref[...].astype(BF16)

    def dot():
        return jnp.dot(a_ref[...], wb_ref[...], preferred_element_type=F32)

    def finish(acc):
        if has_bias:
            acc = acc + b_ref[...]
        if act == "relu2":
            acc = jnp.maximum(acc, 0.0)
            acc = acc * acc
        o_ref[...] = acc.astype(o_ref.dtype)

    if nk == 1:
        finish(dot())
    else:
        rows = pl.ds(pl.multiple_of(m * bm, bm), bm)

        @pl.when(k == 0)
        def _():
            acc_ref[rows, :] = dot()

        @pl.when(jnp.logical_and(k > 0, k < nk - 1))
        def _():
            acc_ref[rows, :] += dot()

        @pl.when(k == nk - 1)
        def _():
            finish(acc_ref[rows, :] + dot())


def _matmul(a, w, layer, bias=None, act=None, out_dtype=F32, bm=1024, bn=512, bk=4096):
    m, kd = a.shape
    _, _, n = w.shape
    bm, bn, bk = min(bm, m), min(bn, n), min(bk, kd)
    nk = kd // bk
    has_bias = bias is not None
    in_specs = [pl.BlockSpec((bm, bk), lambda j, k, i: (i, k)),
                pl.BlockSpec((None, bk, bn), lambda j, k, i: (layer, k, j))]
    args = [a, w]
    if has_bias:
        in_specs.append(pl.BlockSpec((1, bn), lambda j, k, i: (0, j)))
        args.append(bias)
    scratch = [pltpu.VMEM((bk, bn), BF16)]
    if nk > 1:
        scratch.append(pltpu.VMEM((m, bn), F32))
        out_map = lambda j, k, i: (jnp.where(k == nk - 1, i, 0), j)
    else:
        out_map = lambda j, k, i: (i, j)
    return pl.pallas_call(
        functools.partial(_mm_kernel, nk=nk, bm=bm, act=act, has_bias=has_bias),
        grid=(n // bn, nk, m // bm),
        in_specs=in_specs,
        out_specs=pl.BlockSpec((bm, bn), out_map),
        out_shape=jax.ShapeDtypeStruct((m, n), out_dtype),
        scratch_shapes=scratch,
        compiler_params=_params("arbitrary", "arbitrary", "arbitrary"),
        name="matmul",
    )(*args)


def _rotate(x, cos, sin):
    half = x.shape[-1] // 2
    x1, x2 = x[:, :half], x[:, half:]
    return jnp.concatenate([x1 * cos - x2 * sin, x1 * sin + x2 * cos], axis=-1)


def _ret_kernel(lg_ref, q_ref, k_ref, v_ref, g_ref, cos_ref, sin_ref, gn_ref,
                o_ref, yacc_ref, st_ref, *, chunk, nsub, k_scale):
    h, p, i = pl.program_id(0), pl.program_id(1), pl.program_id(2)
    nblk = pl.num_programs(2)
    blk_rows = chunk * nsub
    lgf, lgb = lg_ref[0, h], lg_ref[1, h]

    @pl.when(i == 0)
    def _():
        st_ref[...] = jnp.zeros_like(st_ref)

    r = lax.broadcasted_iota(jnp.int32, (chunk, 1), 0).astype(F32)

    def rotated(ref, rows, scale):
        x = ref[rows, :].astype(F32)
        x = _rotate(x, cos_ref[rows, :], sin_ref[rows, :])
        return x if scale == 1.0 else x * scale

    def state_update(kr, kdec, v, lg):
        kd = (kr * kdec).astype(BF16)
        upd = lax.dot_general(kd, v, (((0,), (0,)), ((), ())), preferred_element_type=F32)
        st_ref[...] = st_ref[...] * jnp.exp(jnp.full((1, 1), chunk * lg, F32)) + upd

    @pl.when(p == 0)
    def _():
        ri = lax.broadcasted_iota(jnp.int32, (chunk, chunk), 0)
        ci = lax.broadcasted_iota(jnp.int32, (chunk, chunk), 1)
        diff = (ri - ci).astype(F32)
        dmat = (jnp.where(diff >= 0, jnp.exp(jnp.maximum(diff, 0.0) * lgf), 0.0)
                + jnp.where(diff <= 0, jnp.exp(jnp.maximum(-diff, 0.0) * lgb), 0.0))
        qdec = jnp.exp((r + 1.0) * lgf)
        kdec = jnp.exp((chunk - 1.0 - r) * lgf)
        for s in range(nsub):
            rows = pl.ds(s * chunk, chunk)
            qr = rotated(q_ref, rows, 1.0)
            kr = rotated(k_ref, rows, k_scale)
            v = v_ref[rows, :]
            sc = lax.dot_general(qr.astype(BF16), kr.astype(BF16), (((1,), (1,)), ((), ())),
                                 preferred_element_type=F32)
            intra = jnp.dot((sc * dmat).astype(BF16), v, preferred_element_type=F32)
            cross = jnp.dot((qr * qdec).astype(BF16), st_ref[...].astype(BF16),
                            preferred_element_type=F32)
            yacc_ref[pl.ds(pl.multiple_of(i * blk_rows + s * chunk, chunk), chunk), :] = intra + cross
            state_update(kr, kdec, v, lgf)

    @pl.when(p == 1)
    def _():
        qdec = jnp.exp((chunk - r) * lgb)
        kdec = jnp.exp(r * lgb)
        base = (nblk - 1 - i) * blk_rows
        for s in reversed(range(nsub)):
            rows = pl.ds(s * chunk, chunk)
            qr = rotated(q_ref, rows, 1.0)
            kr = rotated(k_ref, rows, k_scale)
            v = v_ref[rows, :]
            cross = jnp.dot((qr * qdec).astype(BF16), st_ref[...].astype(BF16),
                            preferred_element_type=F32)
            y = yacc_ref[pl.ds(pl.multiple_of(base + s * chunk, chunk), chunk), :] + cross
            mu = jnp.mean(y, axis=-1, keepdims=True)
            yc = y - mu
            var = jnp.mean(yc * yc, axis=-1, keepdims=True)
            yn = yc * lax.rsqrt(var + EPS) * gn_ref[...]
            g = g_ref[rows, :].astype(F32)
            o_ref[rows, :] = (g * jax.nn.sigmoid(g) * yn).astype(o_ref.dtype)
            state_update(kr, kdec, v, lgb)


def _retention(proj, lg, cos, sin, gn_g, heads):
    s, width = proj.shape
    d = width // 6
    dk, dv = d // heads, 2 * d // heads
    chunk = min(RET_CHUNK, s)
    blk = min(RET_BLOCK, s)
    nblk = s // blk

    def cidx(p, i):
        return i + p * (nblk - 1 - 2 * i)

    grid_spec = pltpu.PrefetchScalarGridSpec(
        num_scalar_prefetch=1,
        grid=(heads, 2, nblk),
        in_specs=[
            pl.BlockSpec((blk, dk), lambda h, p, i, lg: (cidx(p, i), h)),
            pl.BlockSpec((blk, dk), lambda h, p, i, lg: (cidx(p, i), heads + h)),
            pl.BlockSpec((blk, dv), lambda h, p, i, lg: (cidx(p, i), heads + h)),
            pl.BlockSpec((blk, dv), lambda h, p, i, lg: (nblk - 1 - i * p, 2 * heads + h)),
            pl.BlockSpec((blk, dk // 2), lambda h, p, i, lg: (cidx(p, i), 0)),
            pl.BlockSpec((blk, dk // 2), lambda h, p, i, lg: (cidx(p, i), 0)),
            pl.BlockSpec((1, dv), lambda h, p, i, lg: (0, h)),
        ],
        out_specs=pl.BlockSpec((blk, dv), lambda h, p, i, lg: (nblk - 1 - i * p, h)),
        scratch_shapes=[pltpu.VMEM((s, dv), F32), pltpu.VMEM((dk, dv), F32)],
    )
    return pl.pallas_call(
        functools.partial(_ret_kernel, chunk=chunk, nsub=blk // chunk, k_scale=float(dk) ** -0.5),
        grid_spec=grid_spec,
        out_shape=jax.ShapeDtypeStruct((s, 2 * d), BF16),
        compiler_params=_params("arbitrary", "arbitrary", "arbitrary"),
        name="retention",
    )(lg, proj, proj, proj, proj, cos, sin, gn_g)


def _dft_tables(n):
    j = np.arange(n)
    ang = 2.0 * np.pi * ((j[:, None] * j[None, :]) % n) / n
    return np.cos(ang), np.sin(ang)


def _fourier_stage1_kernel(h_ref, cs_ref, ma_ref, mb_ref, wr_ref, wi_ref, o_ref):
    n1, jb, dg = h_ref.shape
    r = jnp.dot(h_ref[...].reshape(n1 * jb, dg), cs_ref[...], preferred_element_type=F32)
    r = jnp.swapaxes(r.reshape(n1, jb, 2 * dg), 0, 1).astype(BF16)
    t_re, t_im = [], []
    for jj in range(jb):
        t = (jnp.dot(ma_ref[...], r[jj, :, :dg], preferred_element_type=F32)
             + jnp.dot(mb_ref[...], r[jj, :, dg:], preferred_element_type=F32))
        tr, ti = t[:n1], t[n1:]
        wr, wi = wr_ref[jj], wi_ref[jj]
        t_re.append(tr * wr - ti * wi)
        t_im.append(tr * wi + ti * wr)
    o_ref[0] = jnp.swapaxes(jnp.stack(t_re), 0, 1).astype(o_ref.dtype)
    o_ref[1] = jnp.swapaxes(jnp.stack(t_im), 0, 1).astype(o_ref.dtype)


def _fourier_stage2_kernel(t_ref, c_ref, s_ref, o_ref):
    kb = t_ref.shape[1]
    res = [jnp.dot(c_ref[...], t_ref[0, kk], preferred_element_type=F32)
           + jnp.dot(s_ref[...], t_ref[1, kk], preferred_element_type=F32) for kk in range(kb)]
    o_ref[...] = jnp.swapaxes(jnp.stack(res), 0, 1).astype(o_ref.dtype)


def _fourier_real(h, groups):
    s, d = h.shape
    dg = d // groups
    n1 = min(DFT_N1, s // 16)
    n2 = s // n1
    tile = 16

    cd, sd = _dft_tables(dg)
    cs = jnp.asarray(np.concatenate([cd, sd], axis=1), F32).astype(BF16)
    c1, s1 = _dft_tables(n1)
    ma = jnp.asarray(np.concatenate([c1, -s1], axis=0), F32).astype(BF16)
    mb = jnp.asarray(np.concatenate([-s1, -c1], axis=0), F32).astype(BF16)
    tw = 2.0 * np.pi * (np.arange(n2)[:, None] * np.arange(n1)[None, :]) / s
    wr = jnp.asarray(np.cos(tw)[:, :, None], F32)
    wi = jnp.asarray(-np.sin(tw)[:, :, None], F32)
    tp = pl.pallas_call(
        _fourier_stage1_kernel,
        grid=(n2 // tile, groups),
        in_specs=[pl.BlockSpec((n1, tile, dg), lambda j, g: (0, j, g)),
                  pl.BlockSpec((dg, 2 * dg), lambda j, g: (0, 0)),
                  pl.BlockSpec((2 * n1, n1), lambda j, g: (0, 0)),
                  pl.BlockSpec((2 * n1, n1), lambda j, g: (0, 0)),
                  pl.BlockSpec((tile, n1, 1), lambda j, g: (j, 0, 0)),
                  pl.BlockSpec((tile, n1, 1), lambda j, g: (j, 0, 0))],
        out_specs=pl.BlockSpec((2, n1, tile, dg), lambda j, g: (0, 0, j, g)),
        out_shape=jax.ShapeDtypeStruct((2, n1, n2, d), BF16),
        compiler_params=_params("parallel", "parallel"),
        name="fourier_stage1",
    )(h.reshape(n1, n2, d), cs, ma, mb, wr, wi)

    c2, s2 = _dft_tables(n2)
    db = min(1024, d)
    mixed = pl.pallas_call(
        _fourier_stage2_kernel,
        grid=(n1 // tile, d // db),
        in_specs=[pl.BlockSpec((2, tile, n2, db), lambda i, e: (0, i, 0, e)),
                  pl.BlockSpec((n2, n2), lambda i, e: (0, 0)),
                  pl.BlockSpec((n2, n2), lambda i, e: (0, 0))],
        out_specs=pl.BlockSpec((n2, tile, db), lambda i, e: (0, i, e)),
        out_shape=jax.ShapeDtypeStruct((n2, n1, d), BF16),
        compiler_params=_params("parallel", "parallel"),
        name="fourier_stage2",
    )(tp, jnp.asarray(c2, F32).astype(BF16), jnp.asarray(s2, F32).astype(BF16))
    return mixed.reshape(s, d)


def kernel(x, c, ada_w, ada_b, norm_g, ret_w_in, ret_w_out, ret_gn_g, ret_decay_fwd,
           ret_decay_bwd, fno_w, fno_b, mlp_w1, mlp_w2):
    batch, s, d = x.shape
    assert batch == 1
    depth = ada_w.shape[0]
    heads, groups = RET_HEADS, FNO_GROUPS
    x = x.reshape(s, d)

    mods = _adaln(c.reshape(d, 1), ada_w, ada_b)

    def mod(layer, idx):
        return mods[layer, :, idx * d:(idx + 1) * d]

    def gain(layer, idx):
        return norm_g[layer, idx].reshape(1, d)

    half = d // heads // 2
    inv = ROPE_BASE ** (-jnp.arange(half, dtype=F32) / half)
    ang = jnp.arange(s, dtype=F32)[:, None] * inv[None, :]
    cos, sin = jnp.cos(ang), jnp.sin(ang)

    (h,) = _row_call(x, gx=gain(0, 0), sc=mod(0, 1), sh=mod(0, 0))
    for layer in range(depth):
        occ = layer // 2
        if layer % 2 == 0:
            proj = _matmul(h, ret_w_in, occ, out_dtype=BF16)
            lg = jnp.stack([jax.nn.log_sigmoid(ret_decay_fwd[occ].astype(F32)),
                            jax.nn.log_sigmoid(ret_decay_bwd[occ].astype(F32))])
            yh = _retention(proj, lg, cos, sin, ret_gn_g[occ].reshape(1, 2 * d), heads)
            y = _matmul(yh, ret_w_out, occ, bk=2048)
        else:
            mixed = _fourier_real(h, groups)
            y = _matmul(mixed, fno_w, occ, bias=fno_b[occ].reshape(1, d))
        x, h = _row_call(x, y=y, gate=mod(layer, 2), gy=gain(layer, 1),
                         gx=gain(layer, 2), sc=mod(layer, 4), sh=mod(layer, 3))
        a = _matmul(h, mlp_w1, layer, act="relu2", out_dtype=BF16)
        y = _matmul(a, mlp_w2, layer, bk=2048)
        if layer + 1 < depth:
            x, h = _row_call(x, y=y, gate=mod(layer, 5), gy=gain(layer, 3),
                             gx=gain(layer + 1, 0), sc=mod(layer + 1, 1), sh=mod(layer + 1, 0))
        else:
            (x,) = _row_call(x, y=y, gate=mod(layer, 5), gy=gain(layer, 3))
    return x.reshape(batch, s, d)
```

```python
import functools

import numpy as np
import jax
import jax.numpy as jnp
from jax import lax
from jax.experimental import pallas as pl
from jax.experimental.pallas import tpu as pltpu

N_MOD = 6
RET_HEADS = 16
FNO_GROUPS = 8
ROPE_BASE = 10000.0
EPS = 1e-6

RET_CHUNK = 256
RET_BLOCK = 2048
DFT_N1 = 64
VMEM_LIMIT_BYTES = 60 * 1024 * 1024

F32 = jnp.float32
BF16 = jnp.bfloat16


def _params(*sem):
    return pltpu.CompilerParams(dimension_semantics=sem, vmem_limit_bytes=VMEM_LIMIT_BYTES)


def _adaln_kernel(c_ref, w_ref, b_ref, o_ref):
    c = c_ref[...]
    s = c * jax.nn.sigmoid(c)
    o_ref[0] = jnp.sum(w_ref[0] * s, axis=0, keepdims=True) + b_ref[0]


def _adaln(c_col, ada_w, ada_b):
    depth, d, n = ada_w.shape
    bn = min(512, n)
    return pl.pallas_call(
        _adaln_kernel,
        grid=(depth, n // bn),
        in_specs=[pl.BlockSpec((d, 1), lambda l, j: (0, 0)),
                  pl.BlockSpec((1, d, bn), lambda l, j: (l, 0, j)),
                  pl.BlockSpec((1, 1, bn), lambda l, j: (l, 0, j))],
        out_specs=pl.BlockSpec((1, 1, bn), lambda l, j: (l, 0, j)),
        out_shape=jax.ShapeDtypeStruct((depth, 1, n), F32),
        compiler_params=_params("parallel", "parallel"),
        name="adaln",
    )(c_col, ada_w, ada_b.reshape(depth, 1, n))


def _rms(x, g):
    ms = jnp.mean(x * x, axis=-1, keepdims=True)
    return x * lax.rsqrt(ms + EPS) * g


def _row_kernel(*refs, has_y, has_h):
    refs = list(refs)
    x_ref = refs.pop(0)
    x = x_ref[...]
    if has_y:
        y_ref, gate_ref, gy_ref = refs.pop(0), refs.pop(0), refs.pop(0)
    if has_h:
        gx_ref, sc_ref, sh_ref = refs.pop(0), refs.pop(0), refs.pop(0)
    if has_y:
        xo_ref = refs.pop(0)
        x = x + gate_ref[...] * _rms(y_ref[...].astype(F32), gy_ref[...])
        xo_ref[...] = x
    if has_h:
        h_ref = refs.pop(0)
        h_ref[...] = (_rms(x, gx_ref[...]) * (1.0 + sc_ref[...]) + sh_ref[...]).astype(BF16)


def _row_call(x, y=None, gate=None, gy=None, gx=None, sc=None, sh=None):
    s, d = x.shape
    bm = min(256, s)
    has_y, has_h = y is not None, gx is not None
    row = pl.BlockSpec((bm, d), lambda i: (i, 0))
    vec = pl.BlockSpec((1, d), lambda i: (0, 0))
    args, in_specs, out_specs, out_shape = [x], [row], [], []
    if has_y:
        args += [y, gate, gy]
        in_specs += [row, vec, vec]
        out_specs.append(row)
        out_shape.append(jax.ShapeDtypeStruct((s, d), F32))
    if has_h:
        args += [gx, sc, sh]
        in_specs += [vec, vec, vec]
        out_specs.append(row)
        out_shape.append(jax.ShapeDtypeStruct((s, d), BF16))
    out = pl.pallas_call(
        functools.partial(_row_kernel, has_y=has_y, has_h=has_h),
        grid=(s // bm,),
        in_specs=in_specs, out_specs=out_specs, out_shape=out_shape,
        compiler_params=_params("parallel"),
        name="row_norm",
    )(*args)
    return out


def _mm_kernel(*refs, nk, bm, act, has_bias):
    refs = list(refs)
    a_ref, w_ref = refs.pop(0), refs.pop(0)
    b_ref = refs.pop(0) if has_bias else None
    o_ref, wb_ref = refs.pop(0), refs.pop(0)
    acc_ref = refs.pop(0) if nk > 1 else None
    k, m = pl.program_id(1), pl.program_id(2)

    @pl.when(m == 0)
    def _():
        wb_ref[...] = w_ref[...].astype(BF16)

    def dot():
        return jnp.dot(a_ref[...], wb_ref[...], preferred_element_type=F32)

    def finish(acc):
        if has_bias:
            acc = acc + b_ref[...]
        if act == "relu2":
            acc = jnp.maximum(acc, 0.0)
            acc = acc * acc
        o_ref[...] = acc.astype(o_ref.dtype)

    if nk == 1:
        finish(dot())
    else:
        rows = pl.ds(pl.multiple_of(m * bm, bm), bm)

        @pl.when(k == 0)
        def _():
            acc_ref[rows, :] = dot()

        @pl.when(jnp.logical_and(k > 0, k < nk - 1))
        def _():
            acc_ref[rows, :] += dot()

        @pl.when(k == nk - 1)
        def _():
            finish(acc_ref[rows, :] + dot())


def _matmul(a, w, layer, bias=None, act=None, out_dtype=F32, bm=1024, bn=512, bk=4096):
    m, kd = a.shape
    _, _, n = w.shape
    bm, bn, bk = min(bm, m), min(bn, n), min(bk, kd)
    nk = kd // bk
    has_bias = bias is not None
    in_specs = [pl.BlockSpec((bm, bk), lambda j, k, i: (i, k)),
                pl.BlockSpec((None, bk, bn), lambda j, k, i: (layer, k, j))]
    args = [a, w]
    if has_bias:
        in_specs.append(pl.BlockSpec((1, bn), lambda j, k, i: (0, j)))
        args.append(bias)
    scratch = [pltpu.VMEM((bk, bn), BF16)]
    if nk > 1:
        scratch.append(pltpu.VMEM((m, bn), F32))
        out_map = lambda j, k, i: (jnp.where(k == nk - 1, i, 0), j)
    else:
        out_map = lambda j, k, i: (i, j)
    return pl.pallas_call(
        functools.partial(_mm_kernel, nk=nk, bm=bm, act=act, has_bias=has_bias),
        grid=(n // bn, nk, m // bm),
        in_specs=in_specs,
        out_specs=pl.BlockSpec((bm, bn), out_map),
        out_shape=jax.ShapeDtypeStruct((m, n), out_dtype),
        scratch_shapes=scratch,
        compiler_params=_params("arbitrary", "arbitrary", "arbitrary"),
        name="matmul",
    )(*args)


def _rotate(x, cos, sin):
    half = x.shape[-1] // 2
    x1, x2 = x[:, :half], x[:, half:]
    return jnp.concatenate([x1 * cos - x2 * sin, x1 * sin + x2 * cos], axis=-1)


def _ret_kernel(lg_ref, q_ref, k_ref, v_ref, g_ref, cos_ref, sin_ref, gn_ref,
                o_ref, yacc_ref, st_ref, *, chunk, nsub, k_scale):
    h, p, i = pl.program_id(0), pl.program_id(1), pl.program_id(2)
    nblk = pl.num_programs(2)
    blk_rows = chunk * nsub
    lgf, lgb = lg_ref[0, h], lg_ref[1, h]

    @pl.when(i == 0)
    def _():
        st_ref[...] = jnp.zeros_like(st_ref)

    r = lax.broadcasted_iota(jnp.int32, (chunk, 1), 0).astype(F32)

    def rotated(ref, rows, scale):
        x = ref[rows, :].astype(F32)
        x = _rotate(x, cos_ref[rows, :], sin_ref[rows, :])
        return x if scale == 1.0 else x * scale

    def state_update(kr, kdec, v, lg):
        kd = (kr * kdec).astype(BF16)
        upd = lax.dot_general(kd, v, (((0,), (0,)), ((), ())), preferred_element_type=F32)
        st_ref[...] = st_ref[...] * jnp.exp(jnp.full((1, 1), chunk * lg, F32)) + upd

    @pl.when(p == 0)
    def _():
        ri = lax.broadcasted_iota(jnp.int32, (chunk, chunk), 0)
        ci = lax.broadcasted_iota(jnp.int32, (chunk, chunk), 1)
        diff = (ri - ci).astype(F32)
        dmat = (jnp.where(diff >= 0, jnp.exp(jnp.maximum(diff, 0.0) * lgf), 0.0)
                + jnp.where(diff <= 0, jnp.exp(jnp.maximum(-diff, 0.0) * lgb), 0.0))
        qdec = jnp.exp((r + 1.0) * lgf)
        kdec = jnp.exp((chunk - 1.0 - r) * lgf)
        for s in range(nsub):
            rows = pl.ds(s * chunk, chunk)
            qr = rotated(q_ref, rows, 1.0)
            kr = rotated(k_ref, rows, k_scale)
            v = v_ref[rows, :]
            sc = lax.dot_general(qr.astype(BF16), kr.astype(BF16), (((1,), (1,)), ((), ())),
                                 preferred_element_type=F32)
            intra = jnp.dot((sc * dmat).astype(BF16), v, preferred_element_type=F32)
            cross = jnp.dot((qr * qdec).astype(BF16), st_ref[...].astype(BF16),
                            preferred_element_type=F32)
            yacc_ref[pl.ds(pl.multiple_of(i * blk_rows + s * chunk, chunk), chunk), :] = intra + cross
            state_update(kr, kdec, v, lgf)

    @pl.when(p == 1)
    def _():
        qdec = jnp.exp((chunk - r) * lgb)
        kdec = jnp.exp(r * lgb)
        base = (nblk - 1 - i) * blk_rows
        for s in reversed(range(nsub)):
            rows = pl.ds(s * chunk, chunk)
            qr = rotated(q_ref, rows, 1.0)
            kr = rotated(k_ref, rows, k_scale)
            v = v_ref[rows, :]
            cross = jnp.dot((qr * qdec).astype(BF16), st_ref[...].astype(BF16),
                            preferred_element_type=F32)
            y = yacc_ref[pl.ds(pl.multiple_of(base + s * chunk, chunk), chunk), :] + cross
            mu = jnp.mean(y, axis=-1, keepdims=True)
            yc = y - mu
            var = jnp.mean(yc * yc, axis=-1, keepdims=True)
            yn = yc * lax.rsqrt(var + EPS) * gn_ref[...]
            g = g_ref[rows, :].astype(F32)
            o_ref[rows, :] = (g * jax.nn.sigmoid(g) * yn).astype(o_ref.dtype)
            state_update(kr, kdec, v, lgb)


def _retention(proj, lg, cos, sin, gn_g, heads):
    s, width = proj.shape
    d = width // 6
    dk, dv = d // heads, 2 * d // heads
    chunk = min(RET_CHUNK, s)
    blk = min(RET_BLOCK, s)
    nblk = s // blk

    def cidx(p, i):
        return i + p * (nblk - 1 - 2 * i)

    grid_spec = pltpu.PrefetchScalarGridSpec(
        num_scalar_prefetch=1,
        grid=(heads, 2, nblk),
        in_specs=[
            pl.BlockSpec((blk, dk), lambda h, p, i, lg: (cidx(p, i), h)),
            pl.BlockSpec((blk, dk), lambda h, p, i, lg: (cidx(p, i), heads + h)),
            pl.BlockSpec((blk, dv), lambda h, p, i, lg: (cidx(p, i), heads + h)),
            pl.BlockSpec((blk, dv), lambda h, p, i, lg: (nblk - 1 - i * p, 2 * heads + h)),
            pl.BlockSpec((blk, dk // 2), lambda h, p, i, lg: (cidx(p, i), 0)),
            pl.BlockSpec((blk, dk // 2), lambda h, p, i, lg: (cidx(p, i), 0)),
            pl.BlockSpec((1, dv), lambda h, p, i, lg: (0, h)),
        ],
        out_specs=pl.BlockSpec((blk, dv), lambda h, p, i, lg: (nblk - 1 - i * p, h)),
        scratch_shapes=[pltpu.VMEM((s, dv), F32), pltpu.VMEM((dk, dv), F32)],
    )
    return pl.pallas_call(
        functools.partial(_ret_kernel, chunk=chunk, nsub=blk // chunk, k_scale=float(dk) ** -0.5),
        grid_spec=grid_spec,
        out_shape=jax.ShapeDtypeStruct((s, 2 * d), BF16),
        compiler_params=_params("arbitrary", "arbitrary", "arbitrary"),
        name="retention",
    )(lg, proj, proj, proj, proj, cos, sin, gn_g)


def _dft_tables(n):
    j = np.arange(n)
    ang = 2.0 * np.pi * ((j[:, None] * j[None, :]) % n) / n
    return np.cos(ang), np.sin(ang)


def _fourier_stage1_kernel(h_ref, cs_ref, ma_ref, mb_ref, wr_ref, wi_ref, o_ref):
    n1, jb, dg = h_ref.shape
    r = jnp.dot(h_ref[...].reshape(n1 * jb, dg), cs_ref[...], preferred_element_type=F32)
    r = jnp.swapaxes(r.reshape(n1, jb, 2 * dg), 0, 1).astype(BF16)
    t_re, t_im = [], []
    for jj in range(jb):
        t = (jnp.dot(ma_ref[...], r[jj, :, :dg], preferred_element_type=F32)
             + jnp.dot(mb_ref[...], r[jj, :, dg:], preferred_element_type=F32))
        tr, ti = t[:n1], t[n1:]
        wr, wi = wr_ref[jj], wi_ref[jj]
        t_re.append(tr * wr - ti * wi)
        t_im.append(tr * wi + ti * wr)
    o_ref[0] = jnp.swapaxes(jnp.stack(t_re), 0, 1).astype(o_ref.dtype)
    o_ref[1] = jnp.swapaxes(jnp.stack(t_im), 0, 1).astype(o_ref.dtype)


def _fourier_stage2_kernel(t_ref, c_ref, s_ref, o_ref):
    kb = t_ref.shape[1]
    res = [jnp.dot(c_ref[...], t_ref[0, kk], preferred_element_type=F32)
           + jnp.dot(s_ref[...], t_ref[1, kk], preferred_element_type=F32) for kk in range(kb)]
    o_ref[...] = jnp.swapaxes(jnp.stack(res), 0, 1).astype(o_ref.dtype)


def _fourier_real(h, groups):
    s, d = h.shape
    dg = d // groups
    n1 = min(DFT_N1, s // 16)
    n2 = s // n1
    tile = 16

    cd, sd = _dft_tables(dg)
    cs = jnp.asarray(np.concatenate([cd, sd], axis=1), F32).astype(BF16)
    c1, s1 = _dft_tables(n1)
    ma = jnp.asarray(np.concatenate([c1, -s1], axis=0), F32).astype(BF16)
    mb = jnp.asarray(np.concatenate([-s1, -c1], axis=0), F32).astype(BF16)
    tw = 2.0 * np.pi * (np.arange(n2)[:, None] * np.arange(n1)[None, :]) / s
    wr = jnp.asarray(np.cos(tw)[:, :, None], F32)
    wi = jnp.asarray(-np.sin(tw)[:, :, None], F32)
    tp = pl.pallas_call(
        _fourier_stage1_kernel,
        grid=(n2 // tile, groups),
        in_specs=[pl.BlockSpec((n1, tile, dg), lambda j, g: (0, j, g)),
                  pl.BlockSpec((dg, 2 * dg), lambda j, g: (0, 0)),
                  pl.BlockSpec((2 * n1, n1), lambda j, g: (0, 0)),
                  pl.BlockSpec((2 * n1, n1), lambda j, g: (0, 0)),
                  pl.BlockSpec((tile, n1, 1), lambda j, g: (j, 0, 0)),
                  pl.BlockSpec((tile, n1, 1), lambda j, g: (j, 0, 0))],
        out_specs=pl.BlockSpec((2, n1, tile, dg), lambda j, g: (0, 0, j, g)),
        out_shape=jax.ShapeDtypeStruct((2, n1, n2, d), BF16),
        compiler_params=_params("parallel", "parallel"),
        name="fourier_stage1",
    )(h.reshape(n1, n2, d), cs, ma, mb, wr, wi)

    c2, s2 = _dft_tables(n2)
    db = min(1024, d)
    mixed = pl.pallas_call(
        _fourier_stage2_kernel,
        grid=(n1 // tile, d // db),
        in_specs=[pl.BlockSpec((2, tile, n2, db), lambda i, e: (0, i, 0, e)),
                  pl.BlockSpec((n2, n2), lambda i, e: (0, 0)),
                  pl.BlockSpec((n2, n2), lambda i, e: (0, 0))],
        out_specs=pl.BlockSpec((n2, tile, db), lambda i, e: (0, i, e)),
        out_shape=jax.ShapeDtypeStruct((n2, n1, d), BF16),
        compiler_params=_params("parallel", "parallel"),
        name="fourier_stage2",
    )(tp, jnp.asarray(c2, F32).astype(BF16), jnp.asarray(s2, F32).astype(BF16))
    return mixed.reshape(s, d)


def kernel(x, c, ada_w, ada_b, norm_g, ret_w_in, ret_w_out, ret_gn_g, ret_decay_fwd,
           ret_decay_bwd, fno_w, fno_b, mlp_w1, mlp_w2):
    batch, s, d = x.shape
    assert batch == 1
    depth = ada_w.shape[0]
    heads, groups = RET_HEADS, FNO_GROUPS
    x = x.reshape(s, d)

    mods = _adaln(c.reshape(d, 1), ada_w, ada_b)

    def mod(layer, idx):
        return mods[layer, :, idx * d:(idx + 1) * d]

    def gain(layer, idx):
        return norm_g[layer, idx].reshape(1, d)

    half = d // heads // 2
    inv = ROPE_BASE ** (-jnp.arange(half, dtype=F32) / half)
    ang = jnp.arange(s, dtype=F32)[:, None] * inv[None, :]
    cos, sin = jnp.cos(ang), jnp.sin(ang)

    (h,) = _row_call(x, gx=gain(0, 0), sc=mod(0, 1), sh=mod(0, 0))
    for layer in range(depth):
        occ = layer // 2
        if layer % 2 == 0:
            proj = _matmul(h, ret_w_in, occ, out_dtype=BF16, bm=512, bn=1024)
            lg = jnp.stack([jax.nn.log_sigmoid(ret_decay_fwd[occ].astype(F32)),
                            jax.nn.log_sigmoid(ret_decay_bwd[occ].astype(F32))])
            yh = _retention(proj, lg, cos, sin, ret_gn_g[occ].reshape(1, 2 * d), heads)
            y = _matmul(yh, ret_w_out, occ, out_dtype=BF16, bm=2048, bk=2048)
        else:
            mixed = _fourier_real(h, groups)
            y = _matmul(mixed, fno_w, occ, bias=fno_b[occ].reshape(1, d), out_dtype=BF16)
        x, h = _row_call(x, y=y, gate=mod(layer, 2), gy=gain(layer, 1),
                         gx=gain(layer, 2), sc=mod(layer, 4), sh=mod(layer, 3))
        if layer == 0:
            a = _matmul(h, mlp_w1, layer, act="relu2", out_dtype=BF16, bm=512, bn=1024)
            y = _matmul(a, mlp_w2, layer, out_dtype=BF16, bm=2048, bk=2048)
        else:
            a = _matmul(h, mlp_w1, layer, act="relu2", out_dtype=BF16)
            y = _matmul(a, mlp_w2, layer, out_dtype=BF16, bk=2048)
        if layer + 1 < depth:
            x, h = _row_call(x, y=y, gate=mod(layer, 5), gy=gain(layer, 3),
                             gx=gain(layer + 1, 0), sc=mod(layer + 1, 1), sh=mod(layer + 1, 0))
        else:
            (x,) = _row_call(x, y=y, gate=mod(layer, 5), gy=gain(layer, 3))
    return x.reshape(batch, s, d)
```

```python
import functools

import numpy as np
import jax
import jax.numpy as jnp
from jax import lax
from jax.experimental import pallas as pl
from jax.experimental.pallas import tpu as pltpu

N_MOD = 6
RET_HEADS = 16
FNO_GROUPS = 8
ROPE_BASE = 10000.0
EPS = 1e-6

RET_CHUNK = 256
RET_BLOCK = 2048
DFT_N1 = 64
VMEM_LIMIT_BYTES = 60 * 1024 * 1024

F32 = jnp.float32
BF16 = jnp.bfloat16


def _params(*sem):
    return pltpu.CompilerParams(dimension_semantics=sem, vmem_limit_bytes=VMEM_LIMIT_BYTES)


def _adaln_kernel(c_ref, w_ref, b_ref, o_ref):
    c = c_ref[...]
    s = c * jax.nn.sigmoid(c)
    o_ref[0] = jnp.sum(w_ref[0] * s, axis=0, keepdims=True) + b_ref[0]


def _adaln(c_col, ada_w, ada_b):
    depth, d, n = ada_w.shape
    bn = min(512, n)
    return pl.pallas_call(
        _adaln_kernel,
        grid=(depth, n // bn),
        in_specs=[pl.BlockSpec((d, 1), lambda l, j: (0, 0)),
                  pl.BlockSpec((1, d, bn), lambda l, j: (l, 0, j)),
                  pl.BlockSpec((1, 1, bn), lambda l, j: (l, 0, j))],
        out_specs=pl.BlockSpec((1, 1, bn), lambda l, j: (l, 0, j)),
        out_shape=jax.ShapeDtypeStruct((depth, 1, n), F32),
        compiler_params=_params("parallel", "parallel"),
        name="adaln",
    )(c_col, ada_w, ada_b.reshape(depth, 1, n))


def _rms(x, g):
    ms = jnp.mean(x * x, axis=-1, keepdims=True)
    return x * lax.rsqrt(ms + EPS) * g


def _row_kernel(*refs, has_y, has_h):
    refs = list(refs)
    x_ref = refs.pop(0)
    x = x_ref[...]
    if has_y:
        y_ref, gate_ref, gy_ref = refs.pop(0), refs.pop(0), refs.pop(0)
    if has_h:
        gx_ref, sc_ref, sh_ref = refs.pop(0), refs.pop(0), refs.pop(0)
    if has_y:
        xo_ref = refs.pop(0)
        x = x + gate_ref[...] * _rms(y_ref[...].astype(F32), gy_ref[...])
        xo_ref[...] = x
    if has_h:
        h_ref = refs.pop(0)
        h_ref[...] = (_rms(x, gx_ref[...]) * (1.0 + sc_ref[...]) + sh_ref[...]).astype(BF16)


def _row_call(x, y=None, gate=None, gy=None, gx=None, sc=None, sh=None):
    s, d = x.shape
    bm = min(256, s)
    has_y, has_h = y is not None, gx is not None
    row = pl.BlockSpec((bm, d), lambda i: (i, 0))
    vec = pl.BlockSpec((1, d), lambda i: (0, 0))
    args, in_specs, out_specs, out_shape = [x], [row], [], []
    if has_y:
        args += [y, gate, gy]
        in_specs += [row, vec, vec]
        out_specs.append(row)
        out_shape.append(jax.ShapeDtypeStruct((s, d), F32))
    if has_h:
        args += [gx, sc, sh]
        in_specs += [vec, vec, vec]
        out_specs.append(row)
        out_shape.append(jax.ShapeDtypeStruct((s, d), BF16))
    out = pl.pallas_call(
        functools.partial(_row_kernel, has_y=has_y, has_h=has_h),
        grid=(s // bm,),
        in_specs=in_specs, out_specs=out_specs, out_shape=out_shape,
        compiler_params=_params("parallel"),
        name="row_norm",
    )(*args)
    return out


def _mm_kernel(*refs, nk, bm, act, has_bias):
    refs = list(refs)
    a_ref, w_ref = refs.pop(0), refs.pop(0)
    b_ref = refs.pop(0) if has_bias else None
    o_ref, wb_ref = refs.pop(0), refs.pop(0)
    acc_ref = refs.pop(0) if nk > 1 else None
    k, m = pl.program_id(1), pl.program_id(2)

    @pl.when(m == 0)
    def _():
        wb_ref[...] = w_ref[...].astype(BF16)

    def dot():
        return jnp.dot(a_ref[...], wb_ref[...], preferred_element_type=F32)

    def finish(acc):
        if has_bias:
            acc = acc + b_ref[...]
        if act == "relu2":
            acc = jnp.maximum(acc, 0.0)
            acc = acc * acc
        o_ref[...] = acc.astype(o_ref.dtype)

    if nk == 1:
        finish(dot())
    else:
        rows = pl.ds(pl.multiple_of(m * bm, bm), bm)

        @pl.when(k == 0)
        def _():
            acc_ref[rows, :] = dot()

        @pl.when(jnp.logical_and(k > 0, k < nk - 1))
        def _():
            acc_ref[rows, :] += dot()

        @pl.when(k == nk - 1)
        def _():
            finish(acc_ref[rows, :] + dot())


def _matmul(a, w, layer, bias=None, act=None, out_dtype=F32, bm=1024, bn=512, bk=4096):
    m, kd = a.shape
    _, _, n = w.shape
    bm, bn, bk = min(bm, m), min(bn, n), min(bk, kd)
    nk = kd // bk
    has_bias = bias is not None
    in_specs = [pl.BlockSpec((bm, bk), lambda j, k, i: (i, k)),
                pl.BlockSpec((None, bk, bn), lambda j, k, i: (layer, k, j))]
    args = [a, w]
    if has_bias:
        in_specs.append(pl.BlockSpec((1, bn), lambda j, k, i: (0, j)))
        args.append(bias)
    scratch = [pltpu.VMEM((bk, bn), BF16)]
    if nk > 1:
        scratch.append(pltpu.VMEM((m, bn), F32))
        out_map = lambda j, k, i: (jnp.where(k == nk - 1, i, 0), j)
    else:
        out_map = lambda j, k, i: (i, j)
    return pl.pallas_call(
        functools.partial(_mm_kernel, nk=nk, bm=bm, act=act, has_bias=has_bias),
        grid=(n // bn, nk, m // bm),
        in_specs=in_specs,
        out_specs=pl.BlockSpec((bm, bn), out_map),
        out_shape=jax.ShapeDtypeStruct((m, n), out_dtype),
        scratch_shapes=scratch,
        compiler_params=_params("arbitrary", "arbitrary", "arbitrary"),
        name="matmul",
    )(*args)


def _mm1_kernel(*refs, nb, ck, act, has_bias, rope):
    refs = list(refs)
    a_ref, w_ref = refs.pop(0), refs.pop(0)
    b_ref = refs.pop(0) if has_bias else None
    cos_ref, sin_ref = (refs.pop(0), refs.pop(0)) if rope else (None, None)
    o_ref, wb_ref = refs
    n, m = pl.program_id(0), pl.program_id(1)

    def cast_next():
        wb_ref[n % 2, pl.ds(pl.multiple_of(m * ck, ck), ck), :] = w_ref[...].astype(BF16)

    def compute(rotary):
        cast_next()
        acc = jnp.dot(a_ref[...], wb_ref[(n + 1) % 2], preferred_element_type=F32)
        if has_bias:
            acc = acc + b_ref[...]
        if act == "relu2":
            acc = jnp.maximum(acc, 0.0)
            acc = acc * acc
        if rotary:
            nq, dk, k_scale = rope
            scale = jnp.where(n - 1 >= nq, k_scale, 1.0)
            c, s = cos_ref[...] * scale, sin_ref[...] * scale
            half = dk // 2
            parts = []
            for hh in range(acc.shape[1] // dk):
                x1 = acc[:, hh * dk:hh * dk + half]
                x2 = acc[:, hh * dk + half:(hh + 1) * dk]
                parts += [x1 * c - x2 * s, x1 * s + x2 * c]
            acc = jnp.concatenate(parts, axis=-1)
        o_ref[...] = acc.astype(o_ref.dtype)

    @pl.when(n == 0)
    def _():
        cast_next()

    if rope:
        nqk = 2 * rope[0]

        @pl.when(jnp.logical_and(n > 0, n - 1 < nqk))
        def _():
            compute(True)

        @pl.when(n - 1 >= nqk)
        def _():
            compute(False)
    else:
        @pl.when(n > 0)
        def _():
            compute(False)


def _matmul1(a, w, layer, bias=None, act=None, out_dtype=BF16, bm=1024, bn=1024, rope=None):
    m, kd = a.shape
    _, _, n = w.shape
    bm, bn = min(bm, m), min(bn, n)
    nb, mb = n // bn, m // bm
    ck = kd // mb
    has_bias = bias is not None

    def row(i, j):
        return jnp.where(i > 0, j, 0)

    in_specs = [pl.BlockSpec((bm, kd), lambda i, j: (row(i, j), 0)),
                pl.BlockSpec((None, ck, bn), lambda i, j: (layer, j, jnp.minimum(i, nb - 1)))]
    args = [a, w]
    if has_bias:
        in_specs.append(pl.BlockSpec((1, bn), lambda i, j: (0, jnp.maximum(i - 1, 0))))
        args.append(bias)
    rope_static = None
    if rope is not None:
        cos, sin, dk, k_scale = rope
        in_specs += [pl.BlockSpec((bm, dk // 2), lambda i, j: (row(i, j), 0))] * 2
        args += [cos, sin]
        rope_static = (dk * (n // 6 // dk) // bn, dk, k_scale)
    return pl.pallas_call(
        functools.partial(_mm1_kernel, nb=nb, ck=ck, act=act, has_bias=has_bias, rope=rope_static),
        grid=(nb + 1, mb),
        in_specs=in_specs,
        out_specs=pl.BlockSpec((bm, bn), lambda i, j: (row(i, j), jnp.maximum(i - 1, 0))),
        out_shape=jax.ShapeDtypeStruct((m, n), out_dtype),
        scratch_shapes=[pltpu.VMEM((2, kd, bn), BF16)],
        compiler_params=_params("arbitrary", "arbitrary"),
        name="matmul1",
    )(*args)


def _ret_kernel(lg_ref, q_ref, k_ref, v_ref, g_ref, gn_ref,
                o_ref, yacc_ref, st_ref, *, chunk, nsub):
    h, p, i = pl.program_id(0), pl.program_id(1), pl.program_id(2)
    nblk = pl.num_programs(2)
    blk_rows = chunk * nsub
    dk = q_ref.shape[1]
    lgf, lgb = lg_ref[0, h], lg_ref[1, h]

    @pl.when(i == 0)
    def _():
        st_ref[...] = jnp.zeros_like(st_ref)

    r = lax.broadcasted_iota(jnp.int32, (chunk, 1), 0).astype(F32)

    def decay(expo):
        return jnp.broadcast_to(jnp.exp(expo), (chunk, dk)).astype(BF16)

    def state_update(k, kdec, v, lg):
        upd = lax.dot_general(k * kdec, v, (((0,), (0,)), ((), ())), preferred_element_type=F32)
        st_ref[...] = st_ref[...] * jnp.exp(jnp.full((1, 1), chunk * lg, F32)) + upd

    @pl.when(p == 0)
    def _():
        ri = lax.broadcasted_iota(jnp.int32, (chunk, chunk), 0)
        ci = lax.broadcasted_iota(jnp.int32, (chunk, chunk), 1)
        diff = (ri - ci).astype(F32)
        dmat = (jnp.where(diff >= 0, jnp.exp(jnp.maximum(diff, 0.0) * lgf), 0.0)
                + jnp.where(diff <= 0, jnp.exp(jnp.maximum(-diff, 0.0) * lgb), 0.0))
        qdec = decay((r + 1.0) * lgf)
        kdec = decay((chunk - 1.0 - r) * lgf)
        for s in range(nsub):
            rows = pl.ds(s * chunk, chunk)
            q, k, v = q_ref[rows, :], k_ref[rows, :], v_ref[rows, :]
            sc = lax.dot_general(q, k, (((1,), (1,)), ((), ())), preferred_element_type=F32)
            intra = jnp.dot((sc * dmat).astype(BF16), v, preferred_element_type=F32)
            cross = jnp.dot(q * qdec, st_ref[...].astype(BF16), preferred_element_type=F32)
            yacc_ref[pl.ds(pl.multiple_of(i * blk_rows + s * chunk, chunk), chunk), :] = intra + cross
            state_update(k, kdec, v, lgf)

    @pl.when(p == 1)
    def _():
        qdec = decay((chunk - r) * lgb)
        kdec = decay(r * lgb)
        base = (nblk - 1 - i) * blk_rows
        for s in reversed(range(nsub)):
            rows = pl.ds(s * chunk, chunk)
            q, k, v = q_ref[rows, :], k_ref[rows, :], v_ref[rows, :]
            cross = jnp.dot(q * qdec, st_ref[...].astype(BF16), preferred_element_type=F32)
            y = yacc_ref[pl.ds(pl.multiple_of(base + s * chunk, chunk), chunk), :] + cross
            mu = jnp.mean(y, axis=-1, keepdims=True)
            yc = y - mu
            var = jnp.mean(yc * yc, axis=-1, keepdims=True)
            yn = yc * lax.rsqrt(var + EPS) * gn_ref[...]
            g = g_ref[rows, :].astype(F32)
            o_ref[rows, :] = (g * jax.nn.sigmoid(g) * yn).astype(o_ref.dtype)
            state_update(k, kdec, v, lgb)


def _retention(proj, lg, gn_g, heads):
    s, width = proj.shape
    d = width // 6
    dk, dv = d // heads, 2 * d // heads
    chunk = min(RET_CHUNK, s)
    blk = min(RET_BLOCK, s)
    nblk = s // blk

    def cidx(p, i):
        return i + p * (nblk - 1 - 2 * i)

    grid_spec = pltpu.PrefetchScalarGridSpec(
        num_scalar_prefetch=1,
        grid=(heads, 2, nblk),
        in_specs=[
            pl.BlockSpec((blk, dk), lambda h, p, i, lg: (cidx(p, i), h)),
            pl.BlockSpec((blk, dk), lambda h, p, i, lg: (cidx(p, i), heads + h)),
            pl.BlockSpec((blk, dv), lambda h, p, i, lg: (cidx(p, i), heads + h)),
            pl.BlockSpec((blk, dv), lambda h, p, i, lg: (nblk - 1 - i * p, 2 * heads + h)),
            pl.BlockSpec((1, dv), lambda h, p, i, lg: (0, h)),
        ],
        out_specs=pl.BlockSpec((blk, dv), lambda h, p, i, lg: (nblk - 1 - i * p, h)),
        scratch_shapes=[pltpu.VMEM((s, dv), F32), pltpu.VMEM((dk, dv), F32)],
    )
    return pl.pallas_call(
        functools.partial(_ret_kernel, chunk=chunk, nsub=blk // chunk),
        grid_spec=grid_spec,
        out_shape=jax.ShapeDtypeStruct((s, 2 * d), BF16),
        compiler_params=_params("arbitrary", "arbitrary", "arbitrary"),
        name="retention",
    )(lg, proj, proj, proj, proj, gn_g)


def _dft_tables(n):
    j = np.arange(n)
    ang = 2.0 * np.pi * ((j[:, None] * j[None, :]) % n) / n
    return np.cos(ang), np.sin(ang)


def _fourier_stage1_kernel(h_ref, cs_ref, ma_ref, mb_ref, wr_ref, wi_ref, o_ref):
    n1, jb, dg = h_ref.shape
    r = jnp.dot(h_ref[...].reshape(n1 * jb, dg), cs_ref[...], preferred_element_type=F32)
    r = jnp.swapaxes(r.reshape(n1, jb, 2 * dg), 0, 1).astype(BF16)
    t_re, t_im = [], []
    for jj in range(jb):
        t = (jnp.dot(ma_ref[...], r[jj, :, :dg], preferred_element_type=F32)
             + jnp.dot(mb_ref[...], r[jj, :, dg:], preferred_element_type=F32))
        tr, ti = t[:n1], t[n1:]
        wr, wi = wr_ref[jj], wi_ref[jj]
        t_re.append(tr * wr - ti * wi)
        t_im.append(tr * wi + ti * wr)
    o_ref[0] = jnp.swapaxes(jnp.stack(t_re), 0, 1).astype(o_ref.dtype)
    o_ref[1] = jnp.swapaxes(jnp.stack(t_im), 0, 1).astype(o_ref.dtype)


def _fourier_stage2_kernel(t_ref, c_ref, s_ref, o_ref):
    kb = t_ref.shape[1]
    res = [jnp.dot(c_ref[...], t_ref[0, kk], preferred_element_type=F32)
           + jnp.dot(s_ref[...], t_ref[1, kk], preferred_element_type=F32) for kk in range(kb)]
    o_ref[...] = jnp.swapaxes(jnp.stack(res), 0, 1).astype(o_ref.dtype)


def _fourier_real(h, groups):
    s, d = h.shape
    dg = d // groups
    n1 = min(DFT_N1, s // 16)
    n2 = s // n1
    tile = 16

    cd, sd = _dft_tables(dg)
    cs = jnp.asarray(np.concatenate([cd, sd], axis=1), F32).astype(BF16)
    c1, s1 = _dft_tables(n1)
    ma = jnp.asarray(np.concatenate([c1, -s1], axis=0), F32).astype(BF16)
    mb = jnp.asarray(np.concatenate([-s1, -c1], axis=0), F32).astype(BF16)
    tw = 2.0 * np.pi * (np.arange(n2)[:, None] * np.arange(n1)[None, :]) / s
    wr = jnp.asarray(np.cos(tw)[:, :, None], F32)
    wi = jnp.asarray(-np.sin(tw)[:, :, None], F32)
    tp = pl.pallas_call(
        _fourier_stage1_kernel,
        grid=(n2 // tile, groups),
        in_specs=[pl.BlockSpec((n1, tile, dg), lambda j, g: (0, j, g)),
                  pl.BlockSpec((dg, 2 * dg), lambda j, g: (0, 0)),
                  pl.BlockSpec((2 * n1, n1), lambda j, g: (0, 0)),
                  pl.BlockSpec((2 * n1, n1), lambda j, g: (0, 0)),
                  pl.BlockSpec((tile, n1, 1), lambda j, g: (j, 0, 0)),
                  pl.BlockSpec((tile, n1, 1), lambda j, g: (j, 0, 0))],
        out_specs=pl.BlockSpec((2, n1, tile, dg), lambda j, g: (0, 0, j, g)),
        out_shape=jax.ShapeDtypeStruct((2, n1, n2, d), BF16),
        compiler_params=_params("parallel", "parallel"),
        name="fourier_stage1",
    )(h.reshape(n1, n2, d), cs, ma, mb, wr, wi)

    c2, s2 = _dft_tables(n2)
    db = min(1024, d)
    mixed = pl.pallas_call(
        _fourier_stage2_kernel,
        grid=(n1 // tile, d // db),
        in_specs=[pl.BlockSpec((2, tile, n2, db), lambda i, e: (0, i, 0, e)),
                  pl.BlockSpec((n2, n2), lambda i, e: (0, 0)),
                  pl.BlockSpec((n2, n2), lambda i, e: (0, 0))],
        out_specs=pl.BlockSpec((n2, tile, db), lambda i, e: (0, i, e)),
        out_shape=jax.ShapeDtypeStruct((n2, n1, d), BF16),
        compiler_params=_params("parallel", "parallel"),
        name="fourier_stage2",
    )(tp, jnp.asarray(c2, F32).astype(BF16), jnp.asarray(s2, F32).astype(BF16))
    return mixed.reshape(s, d)


def kernel(x, c, ada_w, ada_b, norm_g, ret_w_in, ret_w_out, ret_gn_g, ret_decay_fwd,
           ret_decay_bwd, fno_w, fno_b, mlp_w1, mlp_w2):
    batch, s, d = x.shape
    assert batch == 1
    depth = ada_w.shape[0]
    heads, groups = RET_HEADS, FNO_GROUPS
    x = x.reshape(s, d)

    mods = _adaln(c.reshape(d, 1), ada_w, ada_b)

    def mod(layer, idx):
        return mods[layer, :, idx * d:(idx + 1) * d]

    def gain(layer, idx):
        return norm_g[layer, idx].reshape(1, d)

    half = d // heads // 2
    inv = ROPE_BASE ** (-jnp.arange(half, dtype=F32) / half)
    ang = jnp.arange(s, dtype=F32)[:, None] * inv[None, :]
    cos, sin = jnp.cos(ang), jnp.sin(ang)

    (h,) = _row_call(x, gx=gain(0, 0), sc=mod(0, 1), sh=mod(0, 0))
    for layer in range(depth):
        occ = layer // 2
        if layer % 2 == 0:
            dk = d // heads
            proj = _matmul1(h, ret_w_in, occ, rope=(cos, sin, dk, float(dk) ** -0.5))
            lg = jnp.stack([jax.nn.log_sigmoid(ret_decay_fwd[occ].astype(F32)),
                            jax.nn.log_sigmoid(ret_decay_bwd[occ].astype(F32))])
            yh = _retention(proj, lg, ret_gn_g[occ].reshape(1, 2 * d), heads)
            y = _matmul(yh, ret_w_out, occ, out_dtype=BF16, bm=2048, bk=2048)
        else:
            mixed = _fourier_real(h, groups)
            y = _matmul1(mixed, fno_w, occ, bias=fno_b[occ].reshape(1, d))
        x, h = _row_call(x, y=y, gate=mod(layer, 2), gy=gain(layer, 1),
                         gx=gain(layer, 2), sc=mod(layer, 4), sh=mod(layer, 3))
        a = _matmul1(h, mlp_w1, layer, act="relu2")
        y = _matmul(a, mlp_w2, layer, out_dtype=BF16, bm=2048, bk=2048)
        if layer + 1 < depth:
            x, h = _row_call(x, y=y, gate=mod(layer, 5), gy=gain(layer, 3),
                             gx=gain(layer + 1, 0), sc=mod(layer + 1, 1), sh=mod(layer + 1, 0))
        else:
            (x,) = _row_call(x, y=y, gate=mod(layer, 5), gy=gain(layer, 3))
    return x.reshape(batch, s, d)
```

```python
import functools

import numpy as np
import jax
import jax.numpy as jnp
from jax import lax
from jax.experimental import pallas as pl
from jax.experimental.pallas import tpu as pltpu

N_MOD = 6
RET_HEADS = 16
FNO_GROUPS = 8
ROPE_BASE = 10000.0
EPS = 1e-6

RET_CHUNK = 256
RET_BLOCK = 2048
DFT_N1 = 64
VMEM_LIMIT_BYTES = 60 * 1024 * 1024

F32 = jnp.float32
BF16 = jnp.bfloat16


def _params(*sem):
    return pltpu.CompilerParams(dimension_semantics=sem, vmem_limit_bytes=VMEM_LIMIT_BYTES)


def _adaln_kernel(c_ref, w_ref, b_ref, o_ref):
    c = c_ref[...]
    s = c * jax.nn.sigmoid(c)
    o_ref[0] = jnp.sum(w_ref[0] * s, axis=0, keepdims=True) + b_ref[0]


def _adaln(c_col, ada_w, ada_b, layers):
    depth, d, n = ada_w.shape
    bn = min(512, n)
    return pl.pallas_call(
        _adaln_kernel,
        grid=(layers, n // bn),
        in_specs=[pl.BlockSpec((d, 1), lambda l, j: (0, 0)),
                  pl.BlockSpec((1, d, bn), lambda l, j: (l, 0, j)),
                  pl.BlockSpec((1, 1, bn), lambda l, j: (l, 0, j))],
        out_specs=pl.BlockSpec((1, 1, bn), lambda l, j: (l, 0, j)),
        out_shape=jax.ShapeDtypeStruct((layers, 1, n), F32),
        compiler_params=_params("parallel", "parallel"),
        name="adaln",
    )(c_col, ada_w, ada_b.reshape(depth, 1, n))


def _rms(x, g):
    ms = jnp.mean(x * x, axis=-1, keepdims=True)
    return x * lax.rsqrt(ms + EPS) * g


def _row_kernel(*refs, has_y, has_h):
    refs = list(refs)
    x_ref = refs.pop(0)
    x = x_ref[...]
    if has_y:
        y_ref, gate_ref, gy_ref = refs.pop(0), refs.pop(0), refs.pop(0)
    if has_h:
        gx_ref, sc_ref, sh_ref = refs.pop(0), refs.pop(0), refs.pop(0)
    if has_y:
        xo_ref = refs.pop(0)
        x = x + gate_ref[...] * _rms(y_ref[...].astype(F32), gy_ref[...])
        xo_ref[...] = x
    if has_h:
        h_ref = refs.pop(0)
        h_ref[...] = (_rms(x, gx_ref[...]) * (1.0 + sc_ref[...]) + sh_ref[...]).astype(BF16)


def _row_call(x, y=None, gate=None, gy=None, gx=None, sc=None, sh=None):
    s, d = x.shape
    bm = min(256, s)
    has_y, has_h = y is not None, gx is not None
    row = pl.BlockSpec((bm, d), lambda i: (i, 0))
    vec = pl.BlockSpec((1, d), lambda i: (0, 0))
    args, in_specs, out_specs, out_shape = [x], [row], [], []
    if has_y:
        args += [y, gate, gy]
        in_specs += [row, vec, vec]
        out_specs.append(row)
        out_shape.append(jax.ShapeDtypeStruct((s, d), F32))
    if has_h:
        args += [gx, sc, sh]
        in_specs += [vec, vec, vec]
        out_specs.append(row)
        out_shape.append(jax.ShapeDtypeStruct((s, d), BF16))
    out = pl.pallas_call(
        functools.partial(_row_kernel, has_y=has_y, has_h=has_h),
        grid=(s // bm,),
        in_specs=in_specs, out_specs=out_specs, out_shape=out_shape,
        compiler_params=_params("parallel"),
        name="row_norm",
    )(*args)
    return out


def _mm_kernel(*refs, nk, bm, act, has_bias):
    refs = list(refs)
    a_ref, w_ref = refs.pop(0), refs.pop(0)
    b_ref = refs.pop(0) if has_bias else None
    o_ref, wb_ref = refs.pop(0), refs.pop(0)
    acc_ref = refs.pop(0) if nk > 1 else None
    k, m = pl.program_id(1), pl.program_id(2)

    @pl.when(m == 0)
    def _():
        wb_ref[...] = w_ref[...].astype(BF16)

    def dot():
        return jnp.dot(a_ref[...], wb_ref[...], preferred_element_type=F32)

    def finish(acc):
        if has_bias:
            acc = acc + b_ref[...]
        if act == "relu2":
            acc = jnp.maximum(acc, 0.0)
            acc = acc * acc
        o_ref[...] = acc.astype(o_ref.dtype)

    if nk == 1:
        finish(dot())
    else:
        rows = pl.ds(pl.multiple_of(m * bm, bm), bm)

        @pl.when(k == 0)
        def _():
            acc_ref[rows, :] = dot()

        @pl.when(jnp.logical_and(k > 0, k < nk - 1))
        def _():
            acc_ref[rows, :] += dot()

        @pl.when(k == nk - 1)
        def _():
            finish(acc_ref[rows, :] + dot())


def _matmul(a, w, layer, bias=None, act=None, out_dtype=F32, bm=1024, bn=512, bk=4096):
    m, kd = a.shape
    _, _, n = w.shape
    bm, bn, bk = min(bm, m), min(bn, n), min(bk, kd)
    nk = kd // bk
    has_bias = bias is not None
    in_specs = [pl.BlockSpec((bm, bk), lambda j, k, i: (i, k)),
                pl.BlockSpec((None, bk, bn), lambda j, k, i: (layer, k, j))]
    args = [a, w]
    if has_bias:
        in_specs.append(pl.BlockSpec((1, bn), lambda j, k, i: (0, j)))
        args.append(bias)
    scratch = [pltpu.VMEM((bk, bn), BF16)]
    if nk > 1:
        scratch.append(pltpu.VMEM((m, bn), F32))
        out_map = lambda j, k, i: (jnp.where(k == nk - 1, i, 0), j)
    else:
        out_map = lambda j, k, i: (i, j)
    return pl.pallas_call(
        functools.partial(_mm_kernel, nk=nk, bm=bm, act=act, has_bias=has_bias),
        grid=(n // bn, nk, m // bm),
        in_specs=in_specs,
        out_specs=pl.BlockSpec((bm, bn), out_map),
        out_shape=jax.ShapeDtypeStruct((m, n), out_dtype),
        scratch_shapes=scratch,
        compiler_params=_params("arbitrary", "arbitrary", "arbitrary"),
        name="matmul",
    )(*args)


def _mm1_kernel(*refs, nb, ck, act, has_bias, rope, side):
    refs = list(refs)
    a_ref, w_ref = refs.pop(0), refs.pop(0)
    b_ref = refs.pop(0) if has_bias else None
    cos_ref, sin_ref = (refs.pop(0), refs.pop(0)) if rope else (None, None)
    if side:
        c_ref, aw_ref, ab_ref = refs.pop(0), refs.pop(0), refs.pop(0)
        o_ref, mo_ref, wb_ref, sb_ref = refs
    else:
        o_ref, wb_ref = refs
    n, m = pl.program_id(0), pl.program_id(1)

    def cast_next():
        wb_ref[n % 2, pl.ds(pl.multiple_of(m * ck, ck), ck), :] = w_ref[...].astype(BF16)

    def compute(rotary):
        cast_next()
        if side:
            mo_ref[...] = jnp.sum(aw_ref[...] * sb_ref[...], axis=0, keepdims=True) + ab_ref[...]
        acc = jnp.dot(a_ref[...], wb_ref[(n + 1) % 2], preferred_element_type=F32)
        if has_bias:
            acc = acc + b_ref[...]
        if act == "relu2":
            acc = jnp.maximum(acc, 0.0)
            acc = acc * acc
        if rotary:
            nq, dk, k_scale = rope
            scale = jnp.where(n - 1 >= nq, k_scale, 1.0)
            c, s = cos_ref[...] * scale, sin_ref[...] * scale
            half = dk // 2
            parts = []
            for hh in range(acc.shape[1] // dk):
                x1 = acc[:, hh * dk:hh * dk + half]
                x2 = acc[:, hh * dk + half:(hh + 1) * dk]
                parts += [x1 * c - x2 * s, x1 * s + x2 * c]
            acc = jnp.concatenate(parts, axis=-1)
        if len(o_ref.shape) == 3:
            slab = o_ref.shape[2]
            for hh in range(o_ref.shape[0]):
                o_ref[hh] = acc[:, hh * slab:(hh + 1) * slab].astype(o_ref.dtype)
        else:
            o_ref[...] = acc.astype(o_ref.dtype)

    @pl.when(n == 0)
    def _():
        cast_next()

    if side:
        @pl.when(jnp.logical_and(n == 0, m == 0))
        def _():
            c = c_ref[...]
            sb_ref[...] = jnp.broadcast_to(c * jax.nn.sigmoid(c), sb_ref.shape)

    if rope:
        nqk = 2 * rope[0]

        @pl.when(jnp.logical_and(n > 0, n - 1 < nqk))
        def _():
            compute(True)

        @pl.when(n - 1 >= nqk)
        def _():
            compute(False)
    else:
        @pl.when(n > 0)
        def _():
            compute(False)


def _matmul1(a, w, layer, bias=None, act=None, out_dtype=BF16, bm=1024, bn=1024, rope=None,
             slab=None, side=None):
    m, kd = a.shape
    _, _, n = w.shape
    bm, bn = min(bm, m), min(bn, n)
    nb, mb = n // bn, m // bm
    ck = kd // mb
    has_bias = bias is not None

    def row(i, j):
        return jnp.where(i > 0, j, 0)

    def col(i):
        return jnp.maximum(i - 1, 0)

    in_specs = [pl.BlockSpec((bm, kd), lambda i, j: (row(i, j), 0)),
                pl.BlockSpec((None, ck, bn), lambda i, j: (layer, j, jnp.minimum(i, nb - 1)))]
    args = [a, w]
    if has_bias:
        in_specs.append(pl.BlockSpec((1, bn), lambda i, j: (0, col(i))))
        args.append(bias)
    rope_static = None
    if rope is not None:
        cos, sin, dk, k_scale = rope
        in_specs += [pl.BlockSpec((bm, dk // 2), lambda i, j: (row(i, j), 0))] * 2
        args += [cos, sin]
        rope_static = (dk * (n // 6 // dk) // bn, dk, k_scale)
    if slab is None:
        out_specs = [pl.BlockSpec((bm, bn), lambda i, j: (row(i, j), col(i)))]
        out_shape = [jax.ShapeDtypeStruct((m, n), out_dtype)]
    else:
        out_specs = [pl.BlockSpec((bn // slab, bm, slab), lambda i, j: (col(i), row(i, j), 0))]
        out_shape = [jax.ShapeDtypeStruct((n // slab, m, slab), out_dtype)]
    scratch = [pltpu.VMEM((2, kd, bn), BF16)]
    if side is not None:
        c_col, ada_w, ada_b, ada_layer = side
        _, da, na = ada_w.shape
        sw = na // (nb * mb)
        assert sw * nb * mb == na and sw % 128 == 0

        def scol(i, j):
            return col(i) * mb + row(i, j)

        in_specs += [pl.BlockSpec((da, 1), lambda i, j: (0, 0)),
                     pl.BlockSpec((None, da, sw), lambda i, j: (ada_layer, 0, scol(i, j))),
                     pl.BlockSpec((None, 1, sw), lambda i, j: (ada_layer, 0, scol(i, j)))]
        args += [c_col, ada_w, ada_b.reshape(ada_b.shape[0], 1, na)]
        out_specs.append(pl.BlockSpec((1, sw), lambda i, j: (0, scol(i, j))))
        out_shape.append(jax.ShapeDtypeStruct((1, na), F32))
        scratch.append(pltpu.VMEM((da, sw), F32))
    out = pl.pallas_call(
        functools.partial(_mm1_kernel, nb=nb, ck=ck, act=act, has_bias=has_bias, rope=rope_static,
                          side=side is not None),
        grid=(nb + 1, mb),
        in_specs=in_specs,
        out_specs=out_specs,
        out_shape=out_shape,
        scratch_shapes=scratch,
        compiler_params=_params("arbitrary", "arbitrary"),
        name="matmul1",
    )(*args)
    return out if side is not None else out[0]


def _ret_kernel(lg_ref, q_ref, k_ref, v_ref, g_ref, gn_ref,
                o_ref, yacc_ref, st_ref, *, chunk, nsub):
    h, p, i = pl.program_id(0), pl.program_id(1), pl.program_id(2)
    nblk = pl.num_programs(2)
    blk_rows = chunk * nsub
    dk = q_ref.shape[1]
    lgf, lgb = lg_ref[0, h], lg_ref[1, h]

    @pl.when(i == 0)
    def _():
        st_ref[...] = jnp.zeros_like(st_ref)

    r = lax.broadcasted_iota(jnp.int32, (chunk, 1), 0).astype(F32)

    def wide(ref, rows):
        return jnp.concatenate([ref[0, rows, :], ref[1, rows, :]], axis=-1)

    def decay(expo):
        return jnp.broadcast_to(jnp.exp(expo), (chunk, dk)).astype(BF16)

    def state_update(k, kdec, v, lg):
        upd = lax.dot_general(k * kdec, v, (((0,), (0,)), ((), ())), preferred_element_type=F32)
        st_ref[...] = st_ref[...] * jnp.exp(jnp.full((1, 1), chunk * lg, F32)) + upd

    @pl.when(p == 0)
    def _():
        ri = lax.broadcasted_iota(jnp.int32, (chunk, chunk), 0)
        ci = lax.broadcasted_iota(jnp.int32, (chunk, chunk), 1)
        diff = (ri - ci).astype(F32)
        dmat = (jnp.where(diff >= 0, jnp.exp(jnp.maximum(diff, 0.0) * lgf), 0.0)
                + jnp.where(diff <= 0, jnp.exp(jnp.maximum(-diff, 0.0) * lgb), 0.0))
        qdec = decay((r + 1.0) * lgf)
        kdec = decay((chunk - 1.0 - r) * lgf)
        for s in range(nsub):
            rows = pl.ds(s * chunk, chunk)
            q, k, v = q_ref[rows, :], k_ref[rows, :], wide(v_ref, rows)
            sc = lax.dot_general(q, k, (((1,), (1,)), ((), ())), preferred_element_type=F32)
            intra = jnp.dot((sc * dmat).astype(BF16), v, preferred_element_type=F32)
            cross = jnp.dot(q * qdec, st_ref[...].astype(BF16), preferred_element_type=F32)
            yacc_ref[pl.ds(pl.multiple_of(i * blk_rows + s * chunk, chunk), chunk), :] = intra + cross
            state_update(k, kdec, v, lgf)

    @pl.when(p == 1)
    def _():
        qdec = decay((chunk - r) * lgb)
        kdec = decay(r * lgb)
        base = (nblk - 1 - i) * blk_rows
        for s in reversed(range(nsub)):
            rows = pl.ds(s * chunk, chunk)
            q, k, v = q_ref[rows, :], k_ref[rows, :], wide(v_ref, rows)
            cross = jnp.dot(q * qdec, st_ref[...].astype(BF16), preferred_element_type=F32)
            y = yacc_ref[pl.ds(pl.multiple_of(base + s * chunk, chunk), chunk), :] + cross
            mu = jnp.mean(y, axis=-1, keepdims=True)
            yc = y - mu
            var = jnp.mean(yc * yc, axis=-1, keepdims=True)
            yn = yc * lax.rsqrt(var + EPS) * gn_ref[...]
            g = wide(g_ref, rows).astype(F32)
            o_ref[rows, :] = (g * jax.nn.sigmoid(g) * yn).astype(o_ref.dtype)
            state_update(k, kdec, v, lgb)


def _retention(proj, lg, gn_g, heads):
    nslab, s, dk = proj.shape
    d = nslab * dk // 6
    dv = 2 * dk
    chunk = min(RET_CHUNK, s)
    blk = min(RET_BLOCK, s)
    nblk = s // blk

    def cidx(p, i):
        return i + p * (nblk - 1 - 2 * i)

    grid_spec = pltpu.PrefetchScalarGridSpec(
        num_scalar_prefetch=1,
        grid=(heads, 2, nblk),
        in_specs=[
            pl.BlockSpec((None, blk, dk), lambda h, p, i, lg: (h, cidx(p, i), 0)),
            pl.BlockSpec((None, blk, dk), lambda h, p, i, lg: (heads + h, cidx(p, i), 0)),
            pl.BlockSpec((2, blk, dk), lambda h, p, i, lg: (heads + h, cidx(p, i), 0)),
            pl.BlockSpec((2, blk, dk), lambda h, p, i, lg: (2 * heads + h, nblk - 1 - i * p, 0)),
            pl.BlockSpec((1, dv), lambda h, p, i, lg: (0, h)),
        ],
        out_specs=pl.BlockSpec((blk, dv), lambda h, p, i, lg: (nblk - 1 - i * p, h)),
        scratch_shapes=[pltpu.VMEM((s, dv), F32), pltpu.VMEM((dk, dv), F32)],
    )
    return pl.pallas_call(
        functools.partial(_ret_kernel, chunk=chunk, nsub=blk // chunk),
        grid_spec=grid_spec,
        out_shape=jax.ShapeDtypeStruct((s, 2 * d), BF16),
        compiler_params=_params("arbitrary", "arbitrary", "arbitrary"),
        name="retention",
    )(lg, proj, proj, proj, proj, gn_g)


def _dft_tables(n):
    j = np.arange(n)
    ang = 2.0 * np.pi * ((j[:, None] * j[None, :]) % n) / n
    return np.cos(ang), np.sin(ang)


def _fourier_stage1_kernel(h_ref, cs_ref, ma_ref, mb_ref, wr_ref, wi_ref, o_ref):
    n1, jb, dg = h_ref.shape
    r = jnp.dot(h_ref[...].reshape(n1 * jb, dg), cs_ref[...], preferred_element_type=F32)
    r = jnp.swapaxes(r.reshape(n1, jb, 2 * dg), 0, 1).astype(BF16)
    t_re, t_im = [], []
    for jj in range(jb):
        t = (jnp.dot(ma_ref[...], r[jj, :, :dg], preferred_element_type=F32)
             + jnp.dot(mb_ref[...], r[jj, :, dg:], preferred_element_type=F32))
        tr, ti = t[:n1], t[n1:]
        wr, wi = wr_ref[jj], wi_ref[jj]
        t_re.append(tr * wr - ti * wi)
        t_im.append(tr * wi + ti * wr)
    o_ref[0] = jnp.swapaxes(jnp.stack(t_re), 0, 1).astype(o_ref.dtype)
    o_ref[1] = jnp.swapaxes(jnp.stack(t_im), 0, 1).astype(o_ref.dtype)


def _fourier_stage2_kernel(t_ref, c_ref, s_ref, o_ref):
    kb = t_ref.shape[1]
    res = [jnp.dot(c_ref[...], t_ref[0, kk], preferred_element_type=F32)
           + jnp.dot(s_ref[...], t_ref[1, kk], preferred_element_type=F32) for kk in range(kb)]
    o_ref[...] = jnp.swapaxes(jnp.stack(res), 0, 1).astype(o_ref.dtype)


def _fourier_real(h, groups):
    s, d = h.shape
    dg = d // groups
    n1 = min(DFT_N1, s // 16)
    n2 = s // n1
    tile = 16

    cd, sd = _dft_tables(dg)
    cs = jnp.asarray(np.concatenate([cd, sd], axis=1), F32).astype(BF16)
    c1, s1 = _dft_tables(n1)
    ma = jnp.asarray(np.concatenate([c1, -s1], axis=0), F32).astype(BF16)
    mb = jnp.asarray(np.concatenate([-s1, -c1], axis=0), F32).astype(BF16)
    tw = 2.0 * np.pi * (np.arange(n2)[:, None] * np.arange(n1)[None, :]) / s
    wr = jnp.asarray(np.cos(tw)[:, :, None], F32)
    wi = jnp.asarray(-np.sin(tw)[:, :, None], F32)
    tp = pl.pallas_call(
        _fourier_stage1_kernel,
        grid=(n2 // tile, groups),
        in_specs=[pl.BlockSpec((n1, tile, dg), lambda j, g: (0, j, g)),
                  pl.BlockSpec((dg, 2 * dg), lambda j, g: (0, 0)),
                  pl.BlockSpec((2 * n1, n1), lambda j, g: (0, 0)),
                  pl.BlockSpec((2 * n1, n1), lambda j, g: (0, 0)),
                  pl.BlockSpec((tile, n1, 1), lambda j, g: (j, 0, 0)),
                  pl.BlockSpec((tile, n1, 1), lambda j, g: (j, 0, 0))],
        out_specs=pl.BlockSpec((2, n1, tile, dg), lambda j, g: (0, 0, j, g)),
        out_shape=jax.ShapeDtypeStruct((2, n1, n2, d), BF16),
        compiler_params=_params("parallel", "parallel"),
        name="fourier_stage1",
    )(h.reshape(n1, n2, d), cs, ma, mb, wr, wi)

    c2, s2 = _dft_tables(n2)
    db = min(1024, d)
    mixed = pl.pallas_call(
        _fourier_stage2_kernel,
        grid=(n1 // tile, d // db),
        in_specs=[pl.BlockSpec((2, tile, n2, db), lambda i, e: (0, i, 0, e)),
                  pl.BlockSpec((n2, n2), lambda i, e: (0, 0)),
                  pl.BlockSpec((n2, n2), lambda i, e: (0, 0))],
        out_specs=pl.BlockSpec((n2, tile, db), lambda i, e: (0, i, e)),
        out_shape=jax.ShapeDtypeStruct((n2, n1, d), BF16),
        compiler_params=_params("parallel", "parallel"),
        name="fourier_stage2",
    )(tp, jnp.asarray(c2, F32).astype(BF16), jnp.asarray(s2, F32).astype(BF16))
    return mixed.reshape(s, d)


def kernel(x, c, ada_w, ada_b, norm_g, ret_w_in, ret_w_out, ret_gn_g, ret_decay_fwd,
           ret_decay_bwd, fno_w, fno_b, mlp_w1, mlp_w2):
    batch, s, d = x.shape
    assert batch == 1
    depth = ada_w.shape[0]
    heads, groups = RET_HEADS, FNO_GROUPS
    x = x.reshape(s, d)

    assert depth == 2
    c_col = c.reshape(d, 1)
    mods = [_adaln(c_col, ada_w, ada_b, 1)[0], None]

    def mod(layer, idx):
        return mods[layer][:, idx * d:(idx + 1) * d]

    def gain(layer, idx):
        return norm_g[layer, idx].reshape(1, d)

    half = d // heads // 2
    inv = ROPE_BASE ** (-jnp.arange(half, dtype=F32) / half)
    ang = jnp.arange(s, dtype=F32)[:, None] * inv[None, :]
    cos, sin = jnp.cos(ang), jnp.sin(ang)

    (h,) = _row_call(x, gx=gain(0, 0), sc=mod(0, 1), sh=mod(0, 0))
    for layer in range(depth):
        occ = layer // 2
        if layer % 2 == 0:
            dk = d // heads
            proj, mods[1] = _matmul1(h, ret_w_in, occ, rope=(cos, sin, dk, float(dk) ** -0.5),
                                     slab=dk, side=(c_col, ada_w, ada_b, 1))
            lg = jnp.stack([jax.nn.log_sigmoid(ret_decay_fwd[occ].astype(F32)),
                            jax.nn.log_sigmoid(ret_decay_bwd[occ].astype(F32))])
            yh = _retention(proj, lg, ret_gn_g[occ].reshape(1, 2 * d), heads)
            y = _matmul(yh, ret_w_out, occ, out_dtype=BF16, bm=2048, bk=2048)
        else:
            mixed = _fourier_real(h, groups)
            y = _matmul1(mixed, fno_w, occ, bias=fno_b[occ].reshape(1, d))
        x, h = _row_call(x, y=y, gate=mod(layer, 2), gy=gain(layer, 1),
                         gx=gain(layer, 2), sc=mod(layer, 4), sh=mod(layer, 3))
        a = _matmul1(h, mlp_w1, layer, act="relu2")
        y = _matmul(a, mlp_w2, layer, out_dtype=BF16, bm=2048, bk=2048)
        if layer + 1 < depth:
            x, h = _row_call(x, y=y, gate=mod(layer, 5), gy=gain(layer, 3),
                             gx=gain(layer + 1, 0), sc=mod(layer + 1, 1), sh=mod(layer + 1, 0))
        else:
            (x,) = _row_call(x, y=y, gate=mod(layer, 5), gy=gain(layer, 3))
    return x.reshape(batch, s, d)
```

```python
import functools

import numpy as np
import jax
import jax.numpy as jnp
from jax import lax
from jax.experimental import pallas as pl
from jax.experimental.pallas import tpu as pltpu

N_MOD = 6
RET_HEADS = 16
FNO_GROUPS = 8
ROPE_BASE = 10000.0
EPS = 1e-6

RET_CHUNK = 256
RET_BLOCK = 4096
DFT_N1 = 16
VMEM_LIMIT_BYTES = 60 * 1024 * 1024

F32 = jnp.float32
BF16 = jnp.bfloat16


def _params(*sem):
    return pltpu.CompilerParams(dimension_semantics=sem, vmem_limit_bytes=VMEM_LIMIT_BYTES)


def _adaln_kernel(c_ref, w_ref, b_ref, o_ref):
    c = c_ref[...]
    s = c * jax.nn.sigmoid(c)
    o_ref[0] = jnp.sum(w_ref[0] * s, axis=0, keepdims=True) + b_ref[0]


def _adaln(c_col, ada_w, ada_b, layers):
    depth, d, n = ada_w.shape
    bn = min(512, n)
    return pl.pallas_call(
        _adaln_kernel,
        grid=(layers, n // bn),
        in_specs=[pl.BlockSpec((d, 1), lambda l, j: (0, 0)),
                  pl.BlockSpec((1, d, bn), lambda l, j: (l, 0, j)),
                  pl.BlockSpec((1, 1, bn), lambda l, j: (l, 0, j))],
        out_specs=pl.BlockSpec((1, 1, bn), lambda l, j: (l, 0, j)),
        out_shape=jax.ShapeDtypeStruct((layers, 1, n), F32),
        compiler_params=_params("parallel", "parallel"),
        name="adaln",
    )(c_col, ada_w, ada_b.reshape(depth, 1, n))


def _rms(x, g):
    ms = jnp.mean(x * x, axis=-1, keepdims=True)
    return x * lax.rsqrt(ms + EPS) * g


def _row_kernel(*refs, has_y, has_h):
    refs = list(refs)
    x_ref = refs.pop(0)
    x = x_ref[...]
    if has_y:
        y_ref, gate_ref, gy_ref = refs.pop(0), refs.pop(0), refs.pop(0)
    if has_h:
        gx_ref, sc_ref, sh_ref = refs.pop(0), refs.pop(0), refs.pop(0)
    if has_y:
        xo_ref = refs.pop(0)
        x = x + gate_ref[...] * _rms(y_ref[...].astype(F32), gy_ref[...])
        xo_ref[...] = x
    if has_h:
        h_ref = refs.pop(0)
        h_ref[...] = (_rms(x, gx_ref[...]) * (1.0 + sc_ref[...]) + sh_ref[...]).astype(BF16)


def _row_call(x, y=None, gate=None, gy=None, gx=None, sc=None, sh=None):
    s, d = x.shape
    bm = min(512, s)
    has_y, has_h = y is not None, gx is not None
    row = pl.BlockSpec((bm, d), lambda i: (i, 0))
    vec = pl.BlockSpec((1, d), lambda i: (0, 0))
    args, in_specs, out_specs, out_shape = [x], [row], [], []
    if has_y:
        args += [y, gate, gy]
        in_specs += [row, vec, vec]
        out_specs.append(row)
        out_shape.append(jax.ShapeDtypeStruct((s, d), F32))
    if has_h:
        args += [gx, sc, sh]
        in_specs += [vec, vec, vec]
        out_specs.append(row)
        out_shape.append(jax.ShapeDtypeStruct((s, d), BF16))
    out = pl.pallas_call(
        functools.partial(_row_kernel, has_y=has_y, has_h=has_h),
        grid=(s // bm,),
        in_specs=in_specs, out_specs=out_specs, out_shape=out_shape,
        compiler_params=_params("parallel"),
        name="row_norm",
    )(*args)
    return out


def _mm_kernel(*refs, nk, bm, act, has_bias):
    refs = list(refs)
    a_ref, w_ref = refs.pop(0), refs.pop(0)
    b_ref = refs.pop(0) if has_bias else None
    o_ref, wb_ref = refs.pop(0), refs.pop(0)
    acc_ref = refs.pop(0) if nk > 1 else None
    k, m = pl.program_id(1), pl.program_id(2)

    @pl.when(m == 0)
    def _():
        wb_ref[...] = w_ref[...].astype(BF16)

    def dot():
        return jnp.dot(a_ref[...], wb_ref[...], preferred_element_type=F32)

    def finish(acc):
        if has_bias:
            acc = acc + b_ref[...]
        if act == "relu2":
            acc = jnp.maximum(acc, 0.0)
            acc = acc * acc
        o_ref[...] = acc.astype(o_ref.dtype)

    if nk == 1:
        finish(dot())
    else:
        rows = pl.ds(pl.multiple_of(m * bm, bm), bm)

        @pl.when(k == 0)
        def _():
            acc_ref[rows, :] = dot()

        @pl.when(jnp.logical_and(k > 0, k < nk - 1))
        def _():
            acc_ref[rows, :] += dot()

        @pl.when(k == nk - 1)
        def _():
            finish(acc_ref[rows, :] + dot())


def _matmul(a, w, layer, bias=None, act=None, out_dtype=F32, bm=1024, bn=512, bk=4096):
    m, kd = a.shape
    _, _, n = w.shape
    bm, bn, bk = min(bm, m), min(bn, n), min(bk, kd)
    nk = kd // bk
    has_bias = bias is not None
    in_specs = [pl.BlockSpec((bm, bk), lambda j, k, i: (i, k)),
                pl.BlockSpec((None, bk, bn), lambda j, k, i: (layer, k, j))]
    args = [a, w]
    if has_bias:
        in_specs.append(pl.BlockSpec((1, bn), lambda j, k, i: (0, j)))
        args.append(bias)
    scratch = [pltpu.VMEM((bk, bn), BF16)]
    if nk > 1:
        scratch.append(pltpu.VMEM((m, bn), F32))
        out_map = lambda j, k, i: (jnp.where(k == nk - 1, i, 0), j)
    else:
        out_map = lambda j, k, i: (i, j)
    return pl.pallas_call(
        functools.partial(_mm_kernel, nk=nk, bm=bm, act=act, has_bias=has_bias),
        grid=(n // bn, nk, m // bm),
        in_specs=in_specs,
        out_specs=pl.BlockSpec((bm, bn), out_map),
        out_shape=jax.ShapeDtypeStruct((m, n), out_dtype),
        scratch_shapes=scratch,
        compiler_params=_params("arbitrary", "arbitrary", "arbitrary"),
        name="matmul",
    )(*args)


def _mm1_kernel(*refs, nb, ck, act, has_bias, rope, side):
    refs = list(refs)
    a_ref, w_ref = refs.pop(0), refs.pop(0)
    b_ref = refs.pop(0) if has_bias else None
    cos_ref, sin_ref = (refs.pop(0), refs.pop(0)) if rope else (None, None)
    if side:
        c_ref, aw_ref, ab_ref = refs.pop(0), refs.pop(0), refs.pop(0)
        o_ref, mo_ref, wb_ref, sb_ref = refs
    else:
        o_ref, wb_ref = refs
    n, m = pl.program_id(0), pl.program_id(1)

    def cast_next():
        wb_ref[n % 2, pl.ds(pl.multiple_of(m * ck, ck), ck), :] = w_ref[...].astype(BF16)

    def compute(rotary):
        cast_next()
        if side:
            mo_ref[...] = jnp.sum(aw_ref[...] * sb_ref[...], axis=0, keepdims=True) + ab_ref[...]
        acc = jnp.dot(a_ref[...], wb_ref[(n + 1) % 2], preferred_element_type=F32)
        if has_bias:
            acc = acc + b_ref[...]
        if act == "relu2":
            acc = jnp.maximum(acc, 0.0)
            acc = acc * acc
        if rotary:
            nq, dk, k_scale = rope
            scale = jnp.where(n - 1 >= nq, k_scale, 1.0)
            c, s = cos_ref[...] * scale, sin_ref[...] * scale
            half = dk // 2
            parts = []
            for hh in range(acc.shape[1] // dk):
                x1 = acc[:, hh * dk:hh * dk + half]
                x2 = acc[:, hh * dk + half:(hh + 1) * dk]
                parts += [x1 * c - x2 * s, x1 * s + x2 * c]
            acc = jnp.concatenate(parts, axis=-1)
        if len(o_ref.shape) == 3:
            slab = o_ref.shape[2]
            for hh in range(o_ref.shape[0]):
                o_ref[hh] = acc[:, hh * slab:(hh + 1) * slab].astype(o_ref.dtype)
        else:
            o_ref[...] = acc.astype(o_ref.dtype)

    @pl.when(n == 0)
    def _():
        cast_next()

    if side:
        @pl.when(jnp.logical_and(n == 0, m == 0))
        def _():
            c = c_ref[...]
            sb_ref[...] = jnp.broadcast_to(c * jax.nn.sigmoid(c), sb_ref.shape)

    if rope:
        nqk = 2 * rope[0]

        @pl.when(jnp.logical_and(n > 0, n - 1 < nqk))
        def _():
            compute(True)

        @pl.when(n - 1 >= nqk)
        def _():
            compute(False)
    else:
        @pl.when(n > 0)
        def _():
            compute(False)


def _matmul1(a, w, layer, bias=None, act=None, out_dtype=BF16, bm=1024, bn=1024, rope=None,
             slab=None, side=None):
    m, kd = a.shape
    _, _, n = w.shape
    bm, bn = min(bm, m), min(bn, n)
    nb, mb = n // bn, m // bm
    ck = kd // mb
    has_bias = bias is not None

    def row(i, j):
        return jnp.where(i > 0, j, 0)

    def col(i):
        return jnp.maximum(i - 1, 0)

    in_specs = [pl.BlockSpec((bm, kd), lambda i, j: (row(i, j), 0)),
                pl.BlockSpec((None, ck, bn), lambda i, j: (layer, j, jnp.minimum(i, nb - 1)))]
    args = [a, w]
    if has_bias:
        in_specs.append(pl.BlockSpec((1, bn), lambda i, j: (0, col(i))))
        args.append(bias)
    rope_static = None
    if rope is not None:
        cos, sin, dk, k_scale = rope
        in_specs += [pl.BlockSpec((bm, dk // 2), lambda i, j: (row(i, j), 0))] * 2
        args += [cos, sin]
        rope_static = (dk * (n // 6 // dk) // bn, dk, k_scale)
    if slab is None:
        out_specs = [pl.BlockSpec((bm, bn), lambda i, j: (row(i, j), col(i)))]
        out_shape = [jax.ShapeDtypeStruct((m, n), out_dtype)]
    else:
        out_specs = [pl.BlockSpec((bn // slab, bm, slab), lambda i, j: (col(i), row(i, j), 0))]
        out_shape = [jax.ShapeDtypeStruct((n // slab, m, slab), out_dtype)]
    scratch = [pltpu.VMEM((2, kd, bn), BF16)]
    if side is not None:
        c_col, ada_w, ada_b, ada_layer = side
        _, da, na = ada_w.shape
        sw = na // (nb * mb)
        assert sw * nb * mb == na and sw % 128 == 0

        def scol(i, j):
            return col(i) * mb + row(i, j)

        in_specs += [pl.BlockSpec((da, 1), lambda i, j: (0, 0)),
                     pl.BlockSpec((None, da, sw), lambda i, j: (ada_layer, 0, scol(i, j))),
                     pl.BlockSpec((None, 1, sw), lambda i, j: (ada_layer, 0, scol(i, j)))]
        args += [c_col, ada_w, ada_b.reshape(ada_b.shape[0], 1, na)]
        out_specs.append(pl.BlockSpec((1, sw), lambda i, j: (0, scol(i, j))))
        out_shape.append(jax.ShapeDtypeStruct((1, na), F32))
        scratch.append(pltpu.VMEM((da, sw), F32))
    out = pl.pallas_call(
        functools.partial(_mm1_kernel, nb=nb, ck=ck, act=act, has_bias=has_bias, rope=rope_static,
                          side=side is not None),
        grid=(nb + 1, mb),
        in_specs=in_specs,
        out_specs=out_specs,
        out_shape=out_shape,
        scratch_shapes=scratch,
        compiler_params=_params("arbitrary", "arbitrary"),
        name="matmul1",
    )(*args)
    return out if side is not None else out[0]


def _ret_kernel(lg_ref, q_ref, k_ref, v_ref, g_ref, gn_ref,
                o_ref, yacc_ref, st_ref, *, chunk, nsub):
    h, p, i = pl.program_id(0), pl.program_id(1), pl.program_id(2)
    nblk = pl.num_programs(2)
    blk_rows = chunk * nsub
    dk = q_ref.shape[1]
    lgf, lgb = lg_ref[0, h], lg_ref[1, h]

    @pl.when(i == 0)
    def _():
        st_ref[...] = jnp.zeros_like(st_ref)

    r = lax.broadcasted_iota(jnp.int32, (chunk, 1), 0).astype(F32)

    def wide(ref, rows):
        return jnp.concatenate([ref[0, rows, :], ref[1, rows, :]], axis=-1)

    def decay(expo):
        return jnp.broadcast_to(jnp.exp(expo), (chunk, dk)).astype(BF16)

    def state_update(k, kdec, v, lg):
        upd = lax.dot_general(k * kdec, v, (((0,), (0,)), ((), ())), preferred_element_type=F32)
        st_ref[...] = st_ref[...] * jnp.exp(jnp.full((1, 1), chunk * lg, F32)) + upd

    @pl.when(p == 0)
    def _():
        ri = lax.broadcasted_iota(jnp.int32, (chunk, chunk), 0)
        ci = lax.broadcasted_iota(jnp.int32, (chunk, chunk), 1)
        diff = (ri - ci).astype(F32)
        dmat = (jnp.where(diff >= 0, jnp.exp(jnp.maximum(diff, 0.0) * lgf), 0.0)
                + jnp.where(diff <= 0, jnp.exp(jnp.maximum(-diff, 0.0) * lgb), 0.0))
        qdec = decay((r + 1.0) * lgf)
        kdec = decay((chunk - 1.0 - r) * lgf)
        for s in range(nsub):
            rows = pl.ds(s * chunk, chunk)
            q, k, v = q_ref[rows, :], k_ref[rows, :], wide(v_ref, rows)
            sc = lax.dot_general(q, k, (((1,), (1,)), ((), ())), preferred_element_type=F32)
            intra = jnp.dot((sc * dmat).astype(BF16), v, preferred_element_type=F32)
            cross = jnp.dot(q * qdec, st_ref[...].astype(BF16), preferred_element_type=F32)
            yacc_ref[pl.ds(pl.multiple_of(i * blk_rows + s * chunk, chunk), chunk), :] = intra + cross
            state_update(k, kdec, v, lgf)

    @pl.when(p == 1)
    def _():
        qdec = decay((chunk - r) * lgb)
        kdec = decay(r * lgb)
        base = (nblk - 1 - i) * blk_rows
        for s in reversed(range(nsub)):
            rows = pl.ds(s * chunk, chunk)
            q, k, v = q_ref[rows, :], k_ref[rows, :], wide(v_ref, rows)
            cross = jnp.dot(q * qdec, st_ref[...].astype(BF16), preferred_element_type=F32)
            y = yacc_ref[pl.ds(pl.multiple_of(base + s * chunk, chunk), chunk), :] + cross
            mu = jnp.mean(y, axis=-1, keepdims=True)
            yc = y - mu
            var = jnp.mean(yc * yc, axis=-1, keepdims=True)
            yn = yc * lax.rsqrt(var + EPS) * gn_ref[...]
            g = wide(g_ref, rows).astype(F32)
            o_ref[rows, :] = (g * jax.nn.sigmoid(g) * yn).astype(o_ref.dtype)
            state_update(k, kdec, v, lgb)


def _retention(proj, lg, gn_g, heads):
    nslab, s, dk = proj.shape
    d = nslab * dk // 6
    dv = 2 * dk
    chunk = min(RET_CHUNK, s)
    blk = min(RET_BLOCK, s)
    nblk = s // blk

    def cidx(p, i):
        return i + p * (nblk - 1 - 2 * i)

    grid_spec = pltpu.PrefetchScalarGridSpec(
        num_scalar_prefetch=1,
        grid=(heads, 2, nblk),
        in_specs=[
            pl.BlockSpec((None, blk, dk), lambda h, p, i, lg: (h, cidx(p, i), 0)),
            pl.BlockSpec((None, blk, dk), lambda h, p, i, lg: (heads + h, cidx(p, i), 0)),
            pl.BlockSpec((2, blk, dk), lambda h, p, i, lg: (heads + h, cidx(p, i), 0)),
            pl.BlockSpec((2, blk, dk), lambda h, p, i, lg: (2 * heads + h, nblk - 1 - i * p, 0)),
            pl.BlockSpec((1, dv), lambda h, p, i, lg: (0, h)),
        ],
        out_specs=pl.BlockSpec((blk, dv), lambda h, p, i, lg: (nblk - 1 - i * p, h)),
        scratch_shapes=[pltpu.VMEM((s, dv), F32), pltpu.VMEM((dk, dv), F32)],
    )
    return pl.pallas_call(
        functools.partial(_ret_kernel, chunk=chunk, nsub=blk // chunk),
        grid_spec=grid_spec,
        out_shape=jax.ShapeDtypeStruct((s, 2 * d), BF16),
        compiler_params=_params("arbitrary", "arbitrary", "arbitrary"),
        name="retention",
    )(lg, proj, proj, proj, proj, gn_g)


def _dft_tables(n):
    j = np.arange(n)
    ang = 2.0 * np.pi * ((j[:, None] * j[None, :]) % n) / n
    return np.cos(ang), np.sin(ang)


def _fourier_stage1_kernel(h_ref, cs_ref, ma_ref, mb_ref, wr_ref, wi_ref, o_ref, *, tile):
    n1, jb, dg = h_ref.shape
    r = jnp.dot(h_ref[...].reshape(n1 * jb, dg), cs_ref[...], preferred_element_type=F32)
    r = r.astype(BF16).reshape(n1, jb, 2 * dg)
    half = n1 * tile
    for s in range(jb // tile):
        rs = r[:, s * tile:(s + 1) * tile, :].reshape(half, 2 * dg)
        t = (jnp.dot(ma_ref[...], rs[:, :dg], preferred_element_type=F32)
             + jnp.dot(mb_ref[...], rs[:, dg:], preferred_element_type=F32))
        tr, ti = t[:half], t[half:]
        wr, wi = wr_ref[s], wi_ref[s]
        rows = pl.ds(s * tile, tile)
        o_ref[0, :, rows, :] = (tr * wr - ti * wi).astype(o_ref.dtype).reshape(n1, tile, dg)
        o_ref[1, :, rows, :] = (tr * wi + ti * wr).astype(o_ref.dtype).reshape(n1, tile, dg)


def _fourier_stage2_kernel(t_ref, c_ref, s_ref, o_ref):
    n1 = t_ref.shape[1]
    res = [jnp.dot(c_ref[...], t_ref[0, kk], preferred_element_type=F32)
           + jnp.dot(s_ref[...], t_ref[1, kk], preferred_element_type=F32) for kk in range(n1)]
    o_ref[...] = jnp.swapaxes(jnp.stack(res), 0, 1).astype(o_ref.dtype)


def _fourier_real(h, groups):
    s, d = h.shape
    dg = d // groups
    n1 = DFT_N1
    n2 = s // n1
    tile = 16
    jb = min(64, n2)

    cd, sd = _dft_tables(dg)
    cs = jnp.asarray(np.concatenate([cd, sd], axis=1), F32).astype(BF16)
    c1, s1 = _dft_tables(n1)
    eye = np.eye(tile)
    ma = jnp.asarray(np.kron(np.concatenate([c1, -s1], axis=0), eye), F32).astype(BF16)
    mb = jnp.asarray(np.kron(np.concatenate([-s1, -c1], axis=0), eye), F32).astype(BF16)
    tw = 2.0 * np.pi * (np.arange(n1)[None, :, None]
                        * (np.arange(n2 // tile)[:, None, None] * tile + np.arange(tile)[None, None, :])) / s
    tw = tw.reshape(n2 // tile, n1 * tile, 1)
    wr = jnp.asarray(np.cos(tw), F32)
    wi = jnp.asarray(-np.sin(tw), F32)
    tp = pl.pallas_call(
        functools.partial(_fourier_stage1_kernel, tile=tile),
        grid=(n2 // jb, groups),
        in_specs=[pl.BlockSpec((n1, jb, dg), lambda j, g: (0, j, g)),
                  pl.BlockSpec((dg, 2 * dg), lambda j, g: (0, 0)),
                  pl.BlockSpec((2 * n1 * tile, n1 * tile), lambda j, g: (0, 0)),
                  pl.BlockSpec((2 * n1 * tile, n1 * tile), lambda j, g: (0, 0)),
                  pl.BlockSpec((jb // tile, n1 * tile, 1), lambda j, g: (j, 0, 0)),
                  pl.BlockSpec((jb // tile, n1 * tile, 1), lambda j, g: (j, 0, 0))],
        out_specs=pl.BlockSpec((2, n1, jb, dg), lambda j, g: (0, 0, j, g)),
        out_shape=jax.ShapeDtypeStruct((2, n1, n2, d), BF16),
        compiler_params=_params("parallel", "parallel"),
        name="fourier_stage1",
    )(h.reshape(n1, n2, d), cs, ma, mb, wr, wi)

    c2, s2 = _dft_tables(n2)
    db = min(256, d)
    mixed = pl.pallas_call(
        _fourier_stage2_kernel,
        grid=(d // db,),
        in_specs=[pl.BlockSpec((2, n1, n2, db), lambda e: (0, 0, 0, e)),
                  pl.BlockSpec((n2, n2), lambda e: (0, 0)),
                  pl.BlockSpec((n2, n2), lambda e: (0, 0))],
        out_specs=pl.BlockSpec((n2, n1, db), lambda e: (0, 0, e)),
        out_shape=jax.ShapeDtypeStruct((n2, n1, d), BF16),
        compiler_params=_params("parallel"),
        name="fourier_stage2",
    )(tp, jnp.asarray(c2, F32).astype(BF16), jnp.asarray(s2, F32).astype(BF16))
    return mixed.reshape(s, d)


def kernel(x, c, ada_w, ada_b, norm_g, ret_w_in, ret_w_out, ret_gn_g, ret_decay_fwd,
           ret_decay_bwd, fno_w, fno_b, mlp_w1, mlp_w2):
    batch, s, d = x.shape
    assert batch == 1
    depth = ada_w.shape[0]
    heads, groups = RET_HEADS, FNO_GROUPS
    x = x.reshape(s, d)

    assert depth == 2
    c_col = c.reshape(d, 1)
    mods = [_adaln(c_col, ada_w, ada_b, 1)[0], None]

    def mod(layer, idx):
        return mods[layer][:, idx * d:(idx + 1) * d]

    def gain(layer, idx):
        return norm_g[layer, idx].reshape(1, d)

    half = d // heads // 2
    inv = ROPE_BASE ** (-jnp.arange(half, dtype=F32) / half)
    ang = jnp.arange(s, dtype=F32)[:, None] * inv[None, :]
    cos, sin = jnp.cos(ang), jnp.sin(ang)

    (h,) = _row_call(x, gx=gain(0, 0), sc=mod(0, 1), sh=mod(0, 0))
    for layer in range(depth):
        occ = layer // 2
        if layer % 2 == 0:
            dk = d // heads
            proj, mods[1] = _matmul1(h, ret_w_in, occ, rope=(cos, sin, dk, float(dk) ** -0.5),
                                     slab=dk, side=(c_col, ada_w, ada_b, 1))
            lg = jnp.stack([jax.nn.log_sigmoid(ret_decay_fwd[occ].astype(F32)),
                            jax.nn.log_sigmoid(ret_decay_bwd[occ].astype(F32))])
            yh = _retention(proj, lg, ret_gn_g[occ].reshape(1, 2 * d), heads)
            y = _matmul(yh, ret_w_out, occ, out_dtype=BF16, bm=2048, bk=2048)
        else:
            mixed = _fourier_real(h, groups)
            y = _matmul1(mixed, fno_w, occ, bias=fno_b[occ].reshape(1, d))
        x, h = _row_call(x, y=y, gate=mod(layer, 2), gy=gain(layer, 1),
                         gx=gain(layer, 2), sc=mod(layer, 4), sh=mod(layer, 3))
        a = _matmul1(h, mlp_w1, layer, act="relu2")
        y = _matmul(a, mlp_w2, layer, out_dtype=BF16, bm=2048, bk=2048)
        if layer + 1 < depth:
            x, h = _row_call(x, y=y, gate=mod(layer, 5), gy=gain(layer, 3),
                             gx=gain(layer + 1, 0), sc=mod(layer + 1, 1), sh=mod(layer + 1, 0))
        else:
            (x,) = _row_call(x, y=y, gate=mod(layer, 5), gy=gain(layer, 3))
    return x.reshape(batch, s, d)
```

```python
import functools

import numpy as np
import jax
import jax.numpy as jnp
from jax import lax
from jax.experimental import pallas as pl
from jax.experimental.pallas import tpu as pltpu

N_MOD = 6
RET_HEADS = 16
FNO_GROUPS = 8
ROPE_BASE = 10000.0
EPS = 1e-6

RET_CHUNK = 256
RET_BLOCK = 4096
DFT_N1 = 16
VMEM_LIMIT_BYTES = 60 * 1024 * 1024
ROW_SUB = 512

F32 = jnp.float32
BF16 = jnp.bfloat16


def _params(*sem):
    return pltpu.CompilerParams(dimension_semantics=sem, vmem_limit_bytes=VMEM_LIMIT_BYTES)


def _adaln_kernel(c_ref, w_ref, b_ref, o_ref):
    c = c_ref[...]
    s = c * jax.nn.sigmoid(c)
    o_ref[0] = jnp.sum(w_ref[0] * s, axis=0, keepdims=True) + b_ref[0]


def _adaln(c_col, ada_w, ada_b, layers):
    depth, d, n = ada_w.shape
    bn = min(512, n)
    return pl.pallas_call(
        _adaln_kernel,
        grid=(layers, n // bn),
        in_specs=[pl.BlockSpec((d, 1), lambda l, j: (0, 0)),
                  pl.BlockSpec((1, d, bn), lambda l, j: (l, 0, j)),
                  pl.BlockSpec((1, 1, bn), lambda l, j: (l, 0, j))],
        out_specs=pl.BlockSpec((1, 1, bn), lambda l, j: (l, 0, j)),
        out_shape=jax.ShapeDtypeStruct((layers, 1, n), F32),
        compiler_params=_params("parallel", "parallel"),
        name="adaln",
    )(c_col, ada_w, ada_b.reshape(depth, 1, n))


def _rms(x, g):
    ms = jnp.mean(x * x, axis=-1, keepdims=True)
    return x * lax.rsqrt(ms + EPS) * g


def _row_kernel(*refs, has_y, has_h):
    refs = list(refs)
    x_ref = refs.pop(0)
    x = x_ref[...]
    if has_y:
        y_ref, gate_ref, gy_ref = refs.pop(0), refs.pop(0), refs.pop(0)
    if has_h:
        gx_ref, sc_ref, sh_ref = refs.pop(0), refs.pop(0), refs.pop(0)
    if has_y:
        xo_ref = refs.pop(0)
        x = x + gate_ref[...] * _rms(y_ref[...].astype(F32), gy_ref[...])
        xo_ref[...] = x
    if has_h:
        h_ref = refs.pop(0)
        h_ref[...] = (_rms(x, gx_ref[...]) * (1.0 + sc_ref[...]) + sh_ref[...]).astype(BF16)


def _row_call(x, y=None, gate=None, gy=None, gx=None, sc=None, sh=None):
    s, d = x.shape
    bm = min(512, s)
    has_y, has_h = y is not None, gx is not None
    row = pl.BlockSpec((bm, d), lambda i: (i, 0))
    vec = pl.BlockSpec((1, d), lambda i: (0, 0))
    args, in_specs, out_specs, out_shape = [x], [row], [], []
    if has_y:
        args += [y, gate, gy]
        in_specs += [row, vec, vec]
        out_specs.append(row)
        out_shape.append(jax.ShapeDtypeStruct((s, d), F32))
    if has_h:
        args += [gx, sc, sh]
        in_specs += [vec, vec, vec]
        out_specs.append(row)
        out_shape.append(jax.ShapeDtypeStruct((s, d), BF16))
    out = pl.pallas_call(
        functools.partial(_row_kernel, has_y=has_y, has_h=has_h),
        grid=(s // bm,),
        in_specs=in_specs, out_specs=out_specs, out_shape=out_shape,
        compiler_params=_params("parallel"),
        name="row_norm",
    )(*args)
    return out


def _mm_kernel(*refs, nk, bm, act, has_bias):
    refs = list(refs)
    a_ref, w_ref = refs.pop(0), refs.pop(0)
    b_ref = refs.pop(0) if has_bias else None
    o_ref, wb_ref = refs.pop(0), refs.pop(0)
    acc_ref = refs.pop(0) if nk > 1 else None
    k, m = pl.program_id(1), pl.program_id(2)
    rb = min(ROW_SUB, bm)

    @pl.when(m == 0)
    def _():
        wb_ref[...] = w_ref[...].astype(BF16)

    def finish(acc):
        if has_bias:
            acc = acc + b_ref[...]
        if act == "relu2":
            acc = jnp.maximum(acc, 0.0)
            acc = acc * acc
        return acc

    def body(first, last):
        for r in range(bm // rb):
            part = jnp.dot(a_ref[r * rb:(r + 1) * rb, :], wb_ref[...], preferred_element_type=F32)
            rows = pl.ds(pl.multiple_of(m * bm + r * rb, rb), rb)
            if not first:
                part = acc_ref[rows, :] + part
            if last:
                o_ref[r * rb:(r + 1) * rb, :] = finish(part).astype(o_ref.dtype)
            else:
                acc_ref[rows, :] = part

    if nk == 1:
        body(True, True)
    else:
        @pl.when(k == 0)
        def _():
            body(True, False)

        @pl.when(jnp.logical_and(k > 0, k < nk - 1))
        def _():
            body(False, False)

        @pl.when(k == nk - 1)
        def _():
            body(False, True)


def _matmul(a, w, layer, bias=None, act=None, out_dtype=F32, bm=1024, bn=512, bk=4096):
    m, kd = a.shape
    _, _, n = w.shape
    bm, bn, bk = min(bm, m), min(bn, n), min(bk, kd)
    nk = kd // bk
    has_bias = bias is not None
    in_specs = [pl.BlockSpec((bm, bk), lambda j, k, i: (i, k)),
                pl.BlockSpec((None, bk, bn), lambda j, k, i: (layer, k, j))]
    args = [a, w]
    if has_bias:
        in_specs.append(pl.BlockSpec((1, bn), lambda j, k, i: (0, j)))
        args.append(bias)
    scratch = [pltpu.VMEM((bk, bn), BF16)]
    if nk > 1:
        scratch.append(pltpu.VMEM((m, bn), F32))
        out_map = lambda j, k, i: (jnp.where(k == nk - 1, i, 0), j)
    else:
        out_map = lambda j, k, i: (i, j)
    return pl.pallas_call(
        functools.partial(_mm_kernel, nk=nk, bm=bm, act=act, has_bias=has_bias),
        grid=(n // bn, nk, m // bm),
        in_specs=in_specs,
        out_specs=pl.BlockSpec((bm, bn), out_map),
        out_shape=jax.ShapeDtypeStruct((m, n), out_dtype),
        scratch_shapes=scratch,
        compiler_params=_params("arbitrary", "arbitrary", "arbitrary"),
        name="matmul",
    )(*args)


def _mm1_kernel(*refs, nb, ck, act, has_bias, rope, side):
    refs = list(refs)
    a_ref, w_ref = refs.pop(0), refs.pop(0)
    b_ref = refs.pop(0) if has_bias else None
    cos_ref, sin_ref = (refs.pop(0), refs.pop(0)) if rope else (None, None)
    if side:
        c_ref, aw_ref, ab_ref = refs.pop(0), refs.pop(0), refs.pop(0)
        o_ref, mo_ref, wb_ref, sb_ref = refs
    else:
        o_ref, wb_ref = refs
    n, m = pl.program_id(0), pl.program_id(1)

    def cast_next():
        wb_ref[n % 2, pl.ds(pl.multiple_of(m * ck, ck), ck), :] = w_ref[...].astype(BF16)

    def epilogue(acc):
        if has_bias:
            acc = acc + b_ref[...]
        if act == "relu2":
            acc = jnp.maximum(acc, 0.0)
            acc = acc * acc
        return acc

    def compute(rotary):
        cast_next()
        if side:
            mo_ref[...] = jnp.sum(aw_ref[...] * sb_ref[...], axis=0, keepdims=True) + ab_ref[...]
        bm = a_ref.shape[0]
        rb = min(ROW_SUB, bm)
        wb = wb_ref.at[(n + 1) % 2]
        if rotary:
            nq, dk, k_scale = rope
            scale = jnp.where(n - 1 >= nq, k_scale, 1.0)
        for r in range(bm // rb):
            rows = slice(r * rb, (r + 1) * rb)
            acc = epilogue(jnp.dot(a_ref[rows, :], wb[...], preferred_element_type=F32))
            if rotary:
                c, s = cos_ref[rows, :] * scale, sin_ref[rows, :] * scale
                half = dk // 2
                parts = []
                for hh in range(acc.shape[1] // dk):
                    x1 = acc[:, hh * dk:hh * dk + half]
                    x2 = acc[:, hh * dk + half:(hh + 1) * dk]
                    parts += [x1 * c - x2 * s, x1 * s + x2 * c]
                acc = jnp.concatenate(parts, axis=-1)
            if len(o_ref.shape) == 3:
                slab = o_ref.shape[2]
                for hh in range(o_ref.shape[0]):
                    o_ref[hh, rows, :] = acc[:, hh * slab:(hh + 1) * slab].astype(o_ref.dtype)
            else:
                o_ref[rows, :] = acc.astype(o_ref.dtype)

    @pl.when(n == 0)
    def _():
        cast_next()

    if side:
        @pl.when(jnp.logical_and(n == 0, m == 0))
        def _():
            c = c_ref[...]
            sb_ref[...] = jnp.broadcast_to(c * jax.nn.sigmoid(c), sb_ref.shape)

    if rope:
        nqk = 2 * rope[0]

        @pl.when(jnp.logical_and(n > 0, n - 1 < nqk))
        def _():
            compute(True)

        @pl.when(n - 1 >= nqk)
        def _():
            compute(False)
    else:
        @pl.when(n > 0)
        def _():
            compute(False)


def _matmul1(a, w, layer, bias=None, act=None, out_dtype=BF16, bm=1024, bn=1024, rope=None,
             slab=None, side=None):
    m, kd = a.shape
    _, _, n = w.shape
    bm, bn = min(bm, m), min(bn, n)
    nb, mb = n // bn, m // bm
    ck = kd // mb
    has_bias = bias is not None

    def row(i, j):
        return jnp.where(i > 0, j, 0)

    def col(i):
        return jnp.maximum(i - 1, 0)

    in_specs = [pl.BlockSpec((bm, kd), lambda i, j: (row(i, j), 0)),
                pl.BlockSpec((None, ck, bn), lambda i, j: (layer, j, jnp.minimum(i, nb - 1)))]
    args = [a, w]
    if has_bias:
        in_specs.append(pl.BlockSpec((1, bn), lambda i, j: (0, col(i))))
        args.append(bias)
    rope_static = None
    if rope is not None:
        cos, sin, dk, k_scale = rope
        in_specs += [pl.BlockSpec((bm, dk // 2), lambda i, j: (row(i, j), 0))] * 2
        args += [cos, sin]
        rope_static = (dk * (n // 6 // dk) // bn, dk, k_scale)
    if slab is None:
        out_specs = [pl.BlockSpec((bm, bn), lambda i, j: (row(i, j), col(i)))]
        out_shape = [jax.ShapeDtypeStruct((m, n), out_dtype)]
    else:
        out_specs = [pl.BlockSpec((bn // slab, bm, slab), lambda i, j: (col(i), row(i, j), 0))]
        out_shape = [jax.ShapeDtypeStruct((n // slab, m, slab), out_dtype)]
    scratch = [pltpu.VMEM((2, kd, bn), BF16)]
    if side is not None:
        c_col, ada_w, ada_b, ada_layer = side
        _, da, na = ada_w.shape
        sw = na // (nb * mb)
        assert sw * nb * mb == na and sw % 128 == 0

        def scol(i, j):
            return col(i) * mb + row(i, j)

        in_specs += [pl.BlockSpec((da, 1), lambda i, j: (0, 0)),
                     pl.BlockSpec((None, da, sw), lambda i, j: (ada_layer, 0, scol(i, j))),
                     pl.BlockSpec((None, 1, sw), lambda i, j: (ada_layer, 0, scol(i, j)))]
        args += [c_col, ada_w, ada_b.reshape(ada_b.shape[0], 1, na)]
        out_specs.append(pl.BlockSpec((1, sw), lambda i, j: (0, scol(i, j))))
        out_shape.append(jax.ShapeDtypeStruct((1, na), F32))
        scratch.append(pltpu.VMEM((da, sw), F32))
    out = pl.pallas_call(
        functools.partial(_mm1_kernel, nb=nb, ck=ck, act=act, has_bias=has_bias, rope=rope_static,
                          side=side is not None),
        grid=(nb + 1, mb),
        in_specs=in_specs,
        out_specs=out_specs,
        out_shape=out_shape,
        scratch_shapes=scratch,
        compiler_params=_params("arbitrary", "arbitrary"),
        name="matmul1",
    )(*args)
    return out if side is not None else out[0]


def _ret_kernel(lg_ref, q_ref, k_ref, v_ref, g_ref, gn_ref,
                o_ref, yacc_ref, st_ref, *, chunk, nsub):
    h, p, i = pl.program_id(0), pl.program_id(1), pl.program_id(2)
    nblk = pl.num_programs(2)
    blk_rows = chunk * nsub
    dk = q_ref.shape[1]
    lgf, lgb = lg_ref[0, h], lg_ref[1, h]

    @pl.when(i == 0)
    def _():
        st_ref[...] = jnp.zeros_like(st_ref)

    r = lax.broadcasted_iota(jnp.int32, (chunk, 1), 0).astype(F32)

    def wide(ref, rows):
        return jnp.concatenate([ref[0, rows, :], ref[1, rows, :]], axis=-1)

    def decay(expo):
        return jnp.broadcast_to(jnp.exp(expo), (chunk, dk)).astype(BF16)

    def state_update(k, kdec, v, lg):
        upd = lax.dot_general(k * kdec, v, (((0,), (0,)), ((), ())), preferred_element_type=F32)
        st_ref[...] = st_ref[...] * jnp.exp(jnp.full((1, 1), chunk * lg, F32)) + upd

    @pl.when(p == 0)
    def _():
        ri = lax.broadcasted_iota(jnp.int32, (chunk, chunk), 0)
        ci = lax.broadcasted_iota(jnp.int32, (chunk, chunk), 1)
        diff = (ri - ci).astype(F32)
        dmat = (jnp.where(diff >= 0, jnp.exp(jnp.maximum(diff, 0.0) * lgf), 0.0)
                + jnp.where(diff <= 0, jnp.exp(jnp.maximum(-diff, 0.0) * lgb), 0.0))
        qdec = decay((r + 1.0) * lgf)
        kdec = decay((chunk - 1.0 - r) * lgf)
        for s in range(nsub):
            rows = pl.ds(s * chunk, chunk)
            q, k, v = q_ref[rows, :], k_ref[rows, :], wide(v_ref, rows)
            sc = lax.dot_general(q, k, (((1,), (1,)), ((), ())), preferred_element_type=F32)
            intra = jnp.dot((sc * dmat).astype(BF16), v, preferred_element_type=F32)
            cross = jnp.dot(q * qdec, st_ref[...].astype(BF16), preferred_element_type=F32)
            yacc_ref[pl.ds(pl.multiple_of(i * blk_rows + s * chunk, chunk), chunk), :] = intra + cross
            state_update(k, kdec, v, lgf)

    @pl.when(p == 1)
    def _():
        qdec = decay((chunk - r) * lgb)
        kdec = decay(r * lgb)
        base = (nblk - 1 - i) * blk_rows
        for s in reversed(range(nsub)):
            rows = pl.ds(s * chunk, chunk)
            q, k, v = q_ref[rows, :], k_ref[rows, :], wide(v_ref, rows)
            cross = jnp.dot(q * qdec, st_ref[...].astype(BF16), preferred_element_type=F32)
            y = yacc_ref[pl.ds(pl.multiple_of(base + s * chunk, chunk), chunk), :] + cross
            mu = jnp.mean(y, axis=-1, keepdims=True)
            yc = y - mu
            var = jnp.mean(yc * yc, axis=-1, keepdims=True)
            yn = yc * lax.rsqrt(var + EPS) * gn_ref[...]
            g = wide(g_ref, rows).astype(F32)
            o_ref[rows, :] = (g * jax.nn.sigmoid(g) * yn).astype(o_ref.dtype)
            state_update(k, kdec, v, lgb)


def _retention(proj, lg, gn_g, heads):
    nslab, s, dk = proj.shape
    d = nslab * dk // 6
    dv = 2 * dk
    chunk = min(RET_CHUNK, s)
    blk = min(RET_BLOCK, s)
    nblk = s // blk

    def cidx(p, i):
        return i + p * (nblk - 1 - 2 * i)

    grid_spec = pltpu.PrefetchScalarGridSpec(
        num_scalar_prefetch=1,
        grid=(heads, 2, nblk),
        in_specs=[
            pl.BlockSpec((None, blk, dk), lambda h, p, i, lg: (h, cidx(p, i), 0)),
            pl.BlockSpec((None, blk, dk), lambda h, p, i, lg: (heads + h, cidx(p, i), 0)),
            pl.BlockSpec((2, blk, dk), lambda h, p, i, lg: (heads + h, cidx(p, i), 0)),
            pl.BlockSpec((2, blk, dk), lambda h, p, i, lg: (2 * heads + h, nblk - 1 - i * p, 0)),
            pl.BlockSpec((1, dv), lambda h, p, i, lg: (0, h)),
        ],
        out_specs=pl.BlockSpec((blk, dv), lambda h, p, i, lg: (nblk - 1 - i * p, h)),
        scratch_shapes=[pltpu.VMEM((s, dv), F32), pltpu.VMEM((dk, dv), F32)],
    )
    return pl.pallas_call(
        functools.partial(_ret_kernel, chunk=chunk, nsub=blk // chunk),
        grid_spec=grid_spec,
        out_shape=jax.ShapeDtypeStruct((s, 2 * d), BF16),
        compiler_params=_params("arbitrary", "arbitrary", "arbitrary"),
        name="retention",
    )(lg, proj, proj, proj, proj, gn_g)


def _dft_tables(n):
    j = np.arange(n)
    ang = 2.0 * np.pi * ((j[:, None] * j[None, :]) % n) / n
    return np.cos(ang), np.sin(ang)


def _fourier_stage1_kernel(h_ref, cs_ref, ma_ref, mb_ref, wr_ref, wi_ref, o_ref, *, tile):
    n1, jb, dg = h_ref.shape
    r = jnp.dot(h_ref[...].reshape(n1 * jb, dg), cs_ref[...], preferred_element_type=F32)
    r = r.astype(BF16).reshape(n1, jb, 2 * dg)
    half = n1 * tile
    for s in range(jb // tile):
        rs = r[:, s * tile:(s + 1) * tile, :].reshape(half, 2 * dg)
        t = (jnp.dot(ma_ref[...], rs[:, :dg], preferred_element_type=F32)
             + jnp.dot(mb_ref[...], rs[:, dg:], preferred_element_type=F32))
        tr, ti = t[:half], t[half:]
        wr, wi = wr_ref[s], wi_ref[s]
        rows = pl.ds(s * tile, tile)
        o_ref[0, :, rows, :] = (tr * wr - ti * wi).astype(o_ref.dtype).reshape(n1, tile, dg)
        o_ref[1, :, rows, :] = (tr * wi + ti * wr).astype(o_ref.dtype).reshape(n1, tile, dg)


def _fourier_stage2_kernel(t_ref, c_ref, s_ref, o_ref):
    n1 = t_ref.shape[1]
    res = [jnp.dot(c_ref[...], t_ref[0, kk], preferred_element_type=F32)
           + jnp.dot(s_ref[...], t_ref[1, kk], preferred_element_type=F32) for kk in range(n1)]
    o_ref[...] = jnp.swapaxes(jnp.stack(res), 0, 1).astype(o_ref.dtype)


def _fourier_real(h, groups):
    s, d = h.shape
    dg = d // groups
    n1 = DFT_N1
    n2 = s // n1
    tile = 16
    jb = min(64, n2)

    cd, sd = _dft_tables(dg)
    cs = jnp.asarray(np.concatenate([cd, sd], axis=1), F32).astype(BF16)
    c1, s1 = _dft_tables(n1)
    eye = np.eye(tile)
    ma = jnp.asarray(np.kron(np.concatenate([c1, -s1], axis=0), eye), F32).astype(BF16)
    mb = jnp.asarray(np.kron(np.concatenate([-s1, -c1], axis=0), eye), F32).astype(BF16)
    tw = 2.0 * np.pi * (np.arange(n1)[None, :, None]
                        * (np.arange(n2 // tile)[:, None, None] * tile + np.arange(tile)[None, None, :])) / s
    tw = tw.reshape(n2 // tile, n1 * tile, 1)
    wr = jnp.asarray(np.cos(tw), F32)
    wi = jnp.asarray(-np.sin(tw), F32)
    tp = pl.pallas_call(
        functools.partial(_fourier_stage1_kernel, tile=tile),
        grid=(n2 // jb, groups),
        in_specs=[pl.BlockSpec((n1, jb, dg), lambda j, g: (0, j, g)),
                  pl.BlockSpec((dg, 2 * dg), lambda j, g: (0, 0)),
                  pl.BlockSpec((2 * n1 * tile, n1 * tile), lambda j, g: (0, 0)),
                  pl.BlockSpec((2 * n1 * tile, n1 * tile), lambda j, g: (0, 0)),
                  pl.BlockSpec((jb // tile, n1 * tile, 1), lambda j, g: (j, 0, 0)),
                  pl.BlockSpec((jb // tile, n1 * tile, 1), lambda j, g: (j, 0, 0))],
        out_specs=pl.BlockSpec((2, n1, jb, dg), lambda j, g: (0, 0, j, g)),
        out_shape=jax.ShapeDtypeStruct((2, n1, n2, d), BF16),
        compiler_params=_params("parallel", "parallel"),
        name="fourier_stage1",
    )(h.reshape(n1, n2, d), cs, ma, mb, wr, wi)

    c2, s2 = _dft_tables(n2)
    db = min(256, d)
    mixed = pl.pallas_call(
        _fourier_stage2_kernel,
        grid=(d // db,),
        in_specs=[pl.BlockSpec((2, n1, n2, db), lambda e: (0, 0, 0, e)),
                  pl.BlockSpec((n2, n2), lambda e: (0, 0)),
                  pl.BlockSpec((n2, n2), lambda e: (0, 0))],
        out_specs=pl.BlockSpec((n2, n1, db), lambda e: (0, 0, e)),
        out_shape=jax.ShapeDtypeStruct((n2, n1, d), BF16),
        compiler_params=_params("parallel"),
        name="fourier_stage2",
    )(tp, jnp.asarray(c2, F32).astype(BF16), jnp.asarray(s2, F32).astype(BF16))
    return mixed.reshape(s, d)


def kernel(x, c, ada_w, ada_b, norm_g, ret_w_in, ret_w_out, ret_gn_g, ret_decay_fwd,
           ret_decay_bwd, fno_w, fno_b, mlp_w1, mlp_w2):
    batch, s, d = x.shape
    assert batch == 1
    depth = ada_w.shape[0]
    heads, groups = RET_HEADS, FNO_GROUPS
    x = x.reshape(s, d)

    assert depth == 2
    c_col = c.reshape(d, 1)
    mods = [_adaln(c_col, ada_w, ada_b, 1)[0], None]

    def mod(layer, idx):
        return mods[layer][:, idx * d:(idx + 1) * d]

    def gain(layer, idx):
        return norm_g[layer, idx].reshape(1, d)

    half = d // heads // 2
    inv = ROPE_BASE ** (-jnp.arange(half, dtype=F32) / half)
    ang = jnp.arange(s, dtype=F32)[:, None] * inv[None, :]
    cos, sin = jnp.cos(ang), jnp.sin(ang)

    (h,) = _row_call(x, gx=gain(0, 0), sc=mod(0, 1), sh=mod(0, 0))
    for layer in range(depth):
        occ = layer // 2
        if layer % 2 == 0:
            dk = d // heads
            proj, mods[1] = _matmul1(h, ret_w_in, occ, rope=(cos, sin, dk, float(dk) ** -0.5),
                                     slab=dk, side=(c_col, ada_w, ada_b, 1))
            lg = jnp.stack([jax.nn.log_sigmoid(ret_decay_fwd[occ].astype(F32)),
                            jax.nn.log_sigmoid(ret_decay_bwd[occ].astype(F32))])
            yh = _retention(proj, lg, ret_gn_g[occ].reshape(1, 2 * d), heads)
            y = _matmul(yh, ret_w_out, occ, out_dtype=BF16, bm=2048, bk=2048)
        else:
            mixed = _fourier_real(h, groups)
            y = _matmul1(mixed, fno_w, occ, bias=fno_b[occ].reshape(1, d))
        x, h = _row_call(x, y=y, gate=mod(layer, 2), gy=gain(layer, 1),
                         gx=gain(layer, 2), sc=mod(layer, 4), sh=mod(layer, 3))
        a = _matmul1(h, mlp_w1, layer, act="relu2")
        y = _matmul(a, mlp_w2, layer, out_dtype=BF16, bm=2048, bk=2048)
        if layer + 1 < depth:
            x, h = _row_call(x, y=y, gate=mod(layer, 5), gy=gain(layer, 3),
                             gx=gain(layer + 1, 0), sc=mod(layer + 1, 1), sh=mod(layer + 1, 0))
        else:
            (x,) = _row_call(x, y=y, gate=mod(layer, 5), gy=gain(layer, 3))
    return x.reshape(batch, s, d)
```

```python
import functools

import numpy as np
import jax
import jax.numpy as jnp
from jax import lax
from jax.experimental import pallas as pl
from jax.experimental.pallas import tpu as pltpu

N_MOD = 6
RET_HEADS = 16
FNO_GROUPS = 8
ROPE_BASE = 10000.0
EPS = 1e-6

RET_CHUNK = 256
RET_BLOCK = 4096
DFT_N1 = 16
VMEM_LIMIT_BYTES = 60 * 1024 * 1024
ROW_SUB = 512
SIDE_COLS = 256

F32 = jnp.float32
BF16 = jnp.bfloat16


def _params(*sem):
    return pltpu.CompilerParams(dimension_semantics=sem, vmem_limit_bytes=VMEM_LIMIT_BYTES)


def _adaln_kernel(c_ref, w_ref, b_ref, o_ref):
    c = c_ref[...]
    s = c * jax.nn.sigmoid(c)
    o_ref[0] = jnp.sum(w_ref[0] * s, axis=0, keepdims=True) + b_ref[0]


def _adaln(c_col, ada_w, ada_b, ncols):
    depth, d, n = ada_w.shape
    bn = min(512, ncols)
    return pl.pallas_call(
        _adaln_kernel,
        grid=(1, ncols // bn),
        in_specs=[pl.BlockSpec((d, 1), lambda l, j: (0, 0)),
                  pl.BlockSpec((1, d, bn), lambda l, j: (l, 0, j)),
                  pl.BlockSpec((1, 1, bn), lambda l, j: (l, 0, j))],
        out_specs=pl.BlockSpec((1, 1, bn), lambda l, j: (l, 0, j)),
        out_shape=jax.ShapeDtypeStruct((1, 1, ncols), F32),
        compiler_params=_params("parallel", "parallel"),
        name="adaln",
    )(c_col, ada_w, ada_b.reshape(depth, 1, n))


def _rms(x, g):
    ms = jnp.mean(x * x, axis=-1, keepdims=True)
    return x * lax.rsqrt(ms + EPS) * g


def _row_kernel(*refs, has_y, has_h):
    refs = list(refs)
    x_ref = refs.pop(0)
    x = x_ref[...]
    if has_y:
        y_ref, gate_ref, gy_ref = refs.pop(0), refs.pop(0), refs.pop(0)
    if has_h:
        gx_ref, sc_ref, sh_ref = refs.pop(0), refs.pop(0), refs.pop(0)
    if has_y:
        xo_ref = refs.pop(0)
        x = x + gate_ref[...] * _rms(y_ref[...].astype(F32), gy_ref[...])
        xo_ref[...] = x
    if has_h:
        h_ref = refs.pop(0)
        h_ref[...] = (_rms(x, gx_ref[...]) * (1.0 + sc_ref[...]) + sh_ref[...]).astype(BF16)


def _row_call(x, y=None, gate=None, gy=None, gx=None, sc=None, sh=None):
    s, d = x.shape
    bm = min(512, s)
    has_y, has_h = y is not None, gx is not None
    row = pl.BlockSpec((bm, d), lambda i: (i, 0))
    vec = pl.BlockSpec((1, d), lambda i: (0, 0))
    args, in_specs, out_specs, out_shape = [x], [row], [], []
    if has_y:
        args += [y, gate, gy]
        in_specs += [row, vec, vec]
        out_specs.append(row)
        out_shape.append(jax.ShapeDtypeStruct((s, d), F32))
    if has_h:
        args += [gx, sc, sh]
        in_specs += [vec, vec, vec]
        out_specs.append(row)
        out_shape.append(jax.ShapeDtypeStruct((s, d), BF16))
    out = pl.pallas_call(
        functools.partial(_row_kernel, has_y=has_y, has_h=has_h),
        grid=(s // bm,),
        in_specs=in_specs, out_specs=out_specs, out_shape=out_shape,
        compiler_params=_params("parallel"),
        name="row_norm",
    )(*args)
    return out


def _mm_kernel(*refs, nk, bm, act, has_bias):
    refs = list(refs)
    a_ref, w_ref = refs.pop(0), refs.pop(0)
    b_ref = refs.pop(0) if has_bias else None
    o_ref, wb_ref = refs.pop(0), refs.pop(0)
    acc_ref = refs.pop(0) if nk > 1 else None
    k, m = pl.program_id(1), pl.program_id(2)
    rb = min(ROW_SUB, bm)

    @pl.when(m == 0)
    def _():
        wb_ref[...] = w_ref[...].astype(BF16)

    def finish(acc):
        if has_bias:
            acc = acc + b_ref[...]
        if act == "relu2":
            acc = jnp.maximum(acc, 0.0)
            acc = acc * acc
        return acc

    def body(first, last):
        for r in range(bm // rb):
            part = jnp.dot(a_ref[r * rb:(r + 1) * rb, :], wb_ref[...], preferred_element_type=F32)
            rows = pl.ds(pl.multiple_of(m * bm + r * rb, rb), rb)
            if not first:
                part = acc_ref[rows, :] + part
            if last:
                o_ref[r * rb:(r + 1) * rb, :] = finish(part).astype(o_ref.dtype)
            else:
                acc_ref[rows, :] = part

    if nk == 1:
        body(True, True)
    else:
        @pl.when(k == 0)
        def _():
            body(True, False)

        @pl.when(jnp.logical_and(k > 0, k < nk - 1))
        def _():
            body(False, False)

        @pl.when(k == nk - 1)
        def _():
            body(False, True)


def _matmul(a, w, layer, bias=None, act=None, out_dtype=F32, bm=1024, bn=512, bk=4096):
    m, kd = a.shape
    _, _, n = w.shape
    bm, bn, bk = min(bm, m), min(bn, n), min(bk, kd)
    nk = kd // bk
    has_bias = bias is not None
    in_specs = [pl.BlockSpec((bm, bk), lambda j, k, i: (i, k)),
                pl.BlockSpec((None, bk, bn), lambda j, k, i: (layer, k, j))]
    args = [a, w]
    if has_bias:
        in_specs.append(pl.BlockSpec((1, bn), lambda j, k, i: (0, j)))
        args.append(bias)
    scratch = [pltpu.VMEM((bk, bn), BF16)]
    if nk > 1:
        scratch.append(pltpu.VMEM((m, bn), F32))
        out_map = lambda j, k, i: (jnp.where(k == nk - 1, i, 0), j)
    else:
        out_map = lambda j, k, i: (i, j)
    return pl.pallas_call(
        functools.partial(_mm_kernel, nk=nk, bm=bm, act=act, has_bias=has_bias),
        grid=(n // bn, nk, m // bm),
        in_specs=in_specs,
        out_specs=pl.BlockSpec((bm, bn), out_map),
        out_shape=jax.ShapeDtypeStruct((m, n), out_dtype),
        scratch_shapes=scratch,
        compiler_params=_params("arbitrary", "arbitrary", "arbitrary"),
        name="matmul",
    )(*args)


def _mm1_kernel(*refs, nb, ck, act, has_bias, rope, side):
    refs = list(refs)
    a_ref, w_ref = refs.pop(0), refs.pop(0)
    b_ref = refs.pop(0) if has_bias else None
    cos_ref, sin_ref = (refs.pop(0), refs.pop(0)) if rope else (None, None)
    if side:
        c_ref, aw_ref, ab_ref = refs.pop(0), refs.pop(0), refs.pop(0)
        o_ref, mo_ref, wb_ref, sb_ref = refs
    else:
        o_ref, wb_ref = refs
    n, m = pl.program_id(0), pl.program_id(1)

    def cast_next():
        wb_ref[n % 2, pl.ds(pl.multiple_of(m * ck, ck), ck), :] = w_ref[...].astype(BF16)

    def epilogue(acc):
        if has_bias:
            acc = acc + b_ref[...]
        if act == "relu2":
            acc = jnp.maximum(acc, 0.0)
            acc = acc * acc
        return acc

    def compute(rotary):
        cast_next()
        if side:
            lanes = sb_ref.shape[1]
            mv = [jnp.sum(aw_ref[:, t * lanes:(t + 1) * lanes] * sb_ref[...], axis=0, keepdims=True)
                  for t in range(aw_ref.shape[1] // lanes)]
            mo_ref[...] = jnp.concatenate(mv, axis=-1) + ab_ref[...]
        bm = a_ref.shape[0]
        rb = min(ROW_SUB, bm)
        wb = wb_ref.at[(n + 1) % 2]
        if rotary:
            nq, dk, k_scale = rope
            scale = jnp.where(n - 1 >= nq, k_scale, 1.0)
        for r in range(bm // rb):
            rows = slice(r * rb, (r + 1) * rb)
            acc = epilogue(jnp.dot(a_ref[rows, :], wb[...], preferred_element_type=F32))
            if rotary:
                c, s = cos_ref[rows, :] * scale, sin_ref[rows, :] * scale
                half = dk // 2
                parts = []
                for hh in range(acc.shape[1] // dk):
                    x1 = acc[:, hh * dk:hh * dk + half]
                    x2 = acc[:, hh * dk + half:(hh + 1) * dk]
                    parts += [x1 * c - x2 * s, x1 * s + x2 * c]
                acc = jnp.concatenate(parts, axis=-1)
            if len(o_ref.shape) == 3:
                slab = o_ref.shape[2]
                for hh in range(o_ref.shape[0]):
                    o_ref[hh, rows, :] = acc[:, hh * slab:(hh + 1) * slab].astype(o_ref.dtype)
            else:
                o_ref[rows, :] = acc.astype(o_ref.dtype)

    @pl.when(n == 0)
    def _():
        cast_next()

    if side:
        @pl.when(jnp.logical_and(n == 0, m == 0))
        def _():
            c = c_ref[...]
            sb_ref[...] = jnp.broadcast_to(c * jax.nn.sigmoid(c), sb_ref.shape)

    if rope:
        nqk = 2 * rope[0]

        @pl.when(jnp.logical_and(n > 0, n - 1 < nqk))
        def _():
            compute(True)

        @pl.when(n - 1 >= nqk)
        def _():
            compute(False)
    else:
        @pl.when(n > 0)
        def _():
            compute(False)


def _matmul1(a, w, layer, bias=None, act=None, out_dtype=BF16, bm=1024, bn=1024, rope=None,
             slab=None, side=None):
    m, kd = a.shape
    _, _, n = w.shape
    bm, bn = min(bm, m), min(bn, n)
    nb, mb = n // bn, m // bm
    ck = kd // mb
    has_bias = bias is not None

    def row(i, j):
        return jnp.where(i > 0, j, 0)

    def col(i):
        return jnp.maximum(i - 1, 0)

    in_specs = [pl.BlockSpec((bm, kd), lambda i, j: (row(i, j), 0)),
                pl.BlockSpec((None, ck, bn), lambda i, j: (layer, j, jnp.minimum(i, nb - 1)))]
    args = [a, w]
    if has_bias:
        in_specs.append(pl.BlockSpec((1, bn), lambda i, j: (0, col(i))))
        args.append(bias)
    rope_static = None
    if rope is not None:
        cos, sin, dk, k_scale = rope
        in_specs += [pl.BlockSpec((bm, dk // 2), lambda i, j: (row(i, j), 0))] * 2
        args += [cos, sin]
        rope_static = (dk * (n // 6 // dk) // bn, dk, k_scale)
    if slab is None:
        out_specs = [pl.BlockSpec((bm, bn), lambda i, j: (row(i, j), col(i)))]
        out_shape = [jax.ShapeDtypeStruct((m, n), out_dtype)]
    else:
        out_specs = [pl.BlockSpec((bn // slab, bm, slab), lambda i, j: (col(i), row(i, j), 0))]
        out_shape = [jax.ShapeDtypeStruct((n // slab, m, slab), out_dtype)]
    scratch = [pltpu.VMEM((2, kd, bn), BF16)]
    if side is not None:
        c_col, ada_w, ada_b, first_col = side
        layers, da, na = ada_w.shape
        todo = layers * na - first_col
        sw = SIDE_COLS
        nside = todo // sw
        assert nside * sw == todo and first_col % sw == 0 and na % sw == 0 and nside <= nb * mb

        def sblk(i, j):
            return jnp.minimum(col(i) * mb + row(i, j), nside - 1)

        def flat(i, j):
            return first_col // sw + sblk(i, j)

        per_layer = na // sw
        in_specs += [pl.BlockSpec((da, 1), lambda i, j: (0, 0)),
                     pl.BlockSpec((None, da, sw), lambda i, j: (flat(i, j) // per_layer, 0, flat(i, j) % per_layer)),
                     pl.BlockSpec((None, 1, sw), lambda i, j: (flat(i, j) // per_layer, 0, flat(i, j) % per_layer))]
        args += [c_col, ada_w, ada_b.reshape(layers, 1, na)]
        out_specs.append(pl.BlockSpec((1, sw), lambda i, j: (0, sblk(i, j))))
        out_shape.append(jax.ShapeDtypeStruct((1, todo), F32))
        scratch.append(pltpu.VMEM((da, 128), F32))
    out = pl.pallas_call(
        functools.partial(_mm1_kernel, nb=nb, ck=ck, act=act, has_bias=has_bias, rope=rope_static,
                          side=side is not None),
        grid=(nb + 1, mb),
        in_specs=in_specs,
        out_specs=out_specs,
        out_shape=out_shape,
        scratch_shapes=scratch,
        compiler_params=_params("arbitrary", "arbitrary"),
        name="matmul1",
    )(*args)
    return out if side is not None else out[0]


def _ret_kernel(lg_ref, q_ref, k_ref, v_ref, g_ref, gn_ref,
                o_ref, yacc_ref, st_ref, *, chunk, nsub):
    h, p, i = pl.program_id(0), pl.program_id(1), pl.program_id(2)
    nblk = pl.num_programs(2)
    blk_rows = chunk * nsub
    dk = q_ref.shape[1]
    lgf, lgb = lg_ref[0, h], lg_ref[1, h]

    @pl.when(i == 0)
    def _():
        st_ref[...] = jnp.zeros_like(st_ref)

    r = lax.broadcasted_iota(jnp.int32, (chunk, 1), 0).astype(F32)

    def wide(ref, rows):
        return jnp.concatenate([ref[0, rows, :], ref[1, rows, :]], axis=-1)

    def decay(expo):
        return jnp.broadcast_to(jnp.exp(expo), (chunk, dk)).astype(BF16)

    def state_update(k, kdec, v, lg):
        upd = lax.dot_general(k * kdec, v, (((0,), (0,)), ((), ())), preferred_element_type=F32)
        st_ref[...] = st_ref[...] * jnp.exp(jnp.full((1, 1), chunk * lg, F32)) + upd

    @pl.when(p == 0)
    def _():
        ri = lax.broadcasted_iota(jnp.int32, (chunk, chunk), 0)
        ci = lax.broadcasted_iota(jnp.int32, (chunk, chunk), 1)
        diff = (ri - ci).astype(F32)
        dmat = (jnp.where(diff >= 0, jnp.exp(jnp.maximum(diff, 0.0) * lgf), 0.0)
                + jnp.where(diff <= 0, jnp.exp(jnp.maximum(-diff, 0.0) * lgb), 0.0))
        qdec = decay((r + 1.0) * lgf)
        kdec = decay((chunk - 1.0 - r) * lgf)
        for s in range(nsub):
            rows = pl.ds(s * chunk, chunk)
            q, k, v = q_ref[rows, :], k_ref[rows, :], wide(v_ref, rows)
            sc = lax.dot_general(q, k, (((1,), (1,)), ((), ())), preferred_element_type=F32)
            intra = jnp.dot((sc * dmat).astype(BF16), v, preferred_element_type=F32)
            cross = jnp.dot(q * qdec, st_ref[...].astype(BF16), preferred_element_type=F32)
            yacc_ref[pl.ds(pl.multiple_of(i * blk_rows + s * chunk, chunk), chunk), :] = intra + cross
            state_update(k, kdec, v, lgf)

    @pl.when(p == 1)
    def _():
        qdec = decay((chunk - r) * lgb)
        kdec = decay(r * lgb)
        base = (nblk - 1 - i) * blk_rows
        for s in reversed(range(nsub)):
            rows = pl.ds(s * chunk, chunk)
            q, k, v = q_ref[rows, :], k_ref[rows, :], wide(v_ref, rows)
            cross = jnp.dot(q * qdec, st_ref[...].astype(BF16), preferred_element_type=F32)
            y = yacc_ref[pl.ds(pl.multiple_of(base + s * chunk, chunk), chunk), :] + cross
            mu = jnp.mean(y, axis=-1, keepdims=True)
            yc = y - mu
            var = jnp.mean(yc * yc, axis=-1, keepdims=True)
            yn = yc * lax.rsqrt(var + EPS) * gn_ref[...]
            g = wide(g_ref, rows).astype(F32)
            o_ref[rows, :] = (g * jax.nn.sigmoid(g) * yn).astype(o_ref.dtype)
            state_update(k, kdec, v, lgb)


def _retention(proj, lg, gn_g, heads):
    nslab, s, dk = proj.shape
    d = nslab * dk // 6
    dv = 2 * dk
    chunk = min(RET_CHUNK, s)
    blk = min(RET_BLOCK, s)
    nblk = s // blk

    def cidx(p, i):
        return i + p * (nblk - 1 - 2 * i)

    grid_spec = pltpu.PrefetchScalarGridSpec(
        num_scalar_prefetch=1,
        grid=(heads, 2, nblk),
        in_specs=[
            pl.BlockSpec((None, blk, dk), lambda h, p, i, lg: (h, cidx(p, i), 0)),
            pl.BlockSpec((None, blk, dk), lambda h, p, i, lg: (heads + h, cidx(p, i), 0)),
            pl.BlockSpec((2, blk, dk), lambda h, p, i, lg: (heads + h, cidx(p, i), 0)),
            pl.BlockSpec((2, blk, dk), lambda h, p, i, lg: (2 * heads + h, nblk - 1 - i * p, 0)),
            pl.BlockSpec((1, dv), lambda h, p, i, lg: (0, h)),
        ],
        out_specs=pl.BlockSpec((blk, dv), lambda h, p, i, lg: (nblk - 1 - i * p, h)),
        scratch_shapes=[pltpu.VMEM((s, dv), F32), pltpu.VMEM((dk, dv), F32)],
    )
    return pl.pallas_call(
        functools.partial(_ret_kernel, chunk=chunk, nsub=blk // chunk),
        grid_spec=grid_spec,
        out_shape=jax.ShapeDtypeStruct((s, 2 * d), BF16),
        compiler_params=_params("arbitrary", "arbitrary", "arbitrary"),
        name="retention",
    )(lg, proj, proj, proj, proj, gn_g)


def _dft_tables(n):
    j = np.arange(n)
    ang = 2.0 * np.pi * ((j[:, None] * j[None, :]) % n) / n
    return np.cos(ang), np.sin(ang)


def _fourier_stage1_kernel(h_ref, cs_ref, ma_ref, mb_ref, wr_ref, wi_ref, o_ref, *, tile):
    n1, jb, dg = h_ref.shape
    r = jnp.dot(h_ref[...].reshape(n1 * jb, dg), cs_ref[...], preferred_element_type=F32)
    r = r.astype(BF16).reshape(n1, jb, 2 * dg)
    half = n1 * tile
    for s in range(jb // tile):
        rs = r[:, s * tile:(s + 1) * tile, :].reshape(half, 2 * dg)
        t = (jnp.dot(ma_ref[...], rs[:, :dg], preferred_element_type=F32)
             + jnp.dot(mb_ref[...], rs[:, dg:], preferred_element_type=F32))
        tr, ti = t[:half], t[half:]
        wr, wi = wr_ref[s], wi_ref[s]
        rows = pl.ds(s * tile, tile)
        o_ref[0, :, rows, :] = (tr * wr - ti * wi).astype(o_ref.dtype).reshape(n1, tile, dg)
        o_ref[1, :, rows, :] = (tr * wi + ti * wr).astype(o_ref.dtype).reshape(n1, tile, dg)


def _fourier_stage2_kernel(t_ref, c_ref, s_ref, o_ref):
    n1 = t_ref.shape[1]
    res = [jnp.dot(c_ref[...], t_ref[0, kk], preferred_element_type=F32)
           + jnp.dot(s_ref[...], t_ref[1, kk], preferred_element_type=F32) for kk in range(n1)]
    o_ref[...] = jnp.swapaxes(jnp.stack(res), 0, 1).astype(o_ref.dtype)


def _fourier_real(h, groups):
    s, d = h.shape
    dg = d // groups
    n1 = DFT_N1
    n2 = s // n1
    tile = 16
    jb = min(64, n2)

    cd, sd = _dft_tables(dg)
    cs = jnp.asarray(np.concatenate([cd, sd], axis=1), F32).astype(BF16)
    c1, s1 = _dft_tables(n1)
    eye = np.eye(tile)
    ma = jnp.asarray(np.kron(np.concatenate([c1, -s1], axis=0), eye), F32).astype(BF16)
    mb = jnp.asarray(np.kron(np.concatenate([-s1, -c1], axis=0), eye), F32).astype(BF16)
    tw = 2.0 * np.pi * (np.arange(n1)[None, :, None]
                        * (np.arange(n2 // tile)[:, None, None] * tile + np.arange(tile)[None, None, :])) / s
    tw = tw.reshape(n2 // tile, n1 * tile, 1)
    wr = jnp.asarray(np.cos(tw), F32)
    wi = jnp.asarray(-np.sin(tw), F32)
    tp = pl.pallas_call(
        functools.partial(_fourier_stage1_kernel, tile=tile),
        grid=(n2 // jb, groups),
        in_specs=[pl.BlockSpec((n1, jb, dg), lambda j, g: (0, j, g)),
                  pl.BlockSpec((dg, 2 * dg), lambda j, g: (0, 0)),
                  pl.BlockSpec((2 * n1 * tile, n1 * tile), lambda j, g: (0, 0)),
                  pl.BlockSpec((2 * n1 * tile, n1 * tile), lambda j, g: (0, 0)),
                  pl.BlockSpec((jb // tile, n1 * tile, 1), lambda j, g: (j, 0, 0)),
                  pl.BlockSpec((jb // tile, n1 * tile, 1), lambda j, g: (j, 0, 0))],
        out_specs=pl.BlockSpec((2, n1, jb, dg), lambda j, g: (0, 0, j, g)),
        out_shape=jax.ShapeDtypeStruct((2, n1, n2, d), BF16),
        compiler_params=_params("parallel", "parallel"),
        name="fourier_stage1",
    )(h.reshape(n1, n2, d), cs, ma, mb, wr, wi)

    c2, s2 = _dft_tables(n2)
    db = min(256, d)
    mixed = pl.pallas_call(
        _fourier_stage2_kernel,
        grid=(d // db,),
        in_specs=[pl.BlockSpec((2, n1, n2, db), lambda e: (0, 0, 0, e)),
                  pl.BlockSpec((n2, n2), lambda e: (0, 0)),
                  pl.BlockSpec((n2, n2), lambda e: (0, 0))],
        out_specs=pl.BlockSpec((n2, n1, db), lambda e: (0, 0, e)),
        out_shape=jax.ShapeDtypeStruct((n2, n1, d), BF16),
        compiler_params=_params("parallel"),
        name="fourier_stage2",
    )(tp, jnp.asarray(c2, F32).astype(BF16), jnp.asarray(s2, F32).astype(BF16))
    return mixed.reshape(s, d)


def kernel(x, c, ada_w, ada_b, norm_g, ret_w_in, ret_w_out, ret_gn_g, ret_decay_fwd,
           ret_decay_bwd, fno_w, fno_b, mlp_w1, mlp_w2):
    batch, s, d = x.shape
    assert batch == 1
    depth = ada_w.shape[0]
    heads, groups = RET_HEADS, FNO_GROUPS
    x = x.reshape(s, d)

    assert depth == 2
    c_col = c.reshape(d, 1)
    mods = [_adaln(c_col, ada_w, ada_b, 2 * d)[0], None]

    def mod(layer, idx):
        return mods[layer][:, idx * d:(idx + 1) * d]

    def gain(layer, idx):
        return norm_g[layer, idx].reshape(1, d)

    half = d // heads // 2
    inv = ROPE_BASE ** (-jnp.arange(half, dtype=F32) / half)
    ang = jnp.arange(s, dtype=F32)[:, None] * inv[None, :]
    cos, sin = jnp.cos(ang), jnp.sin(ang)

    (h,) = _row_call(x, gx=gain(0, 0), sc=mod(0, 1), sh=mod(0, 0))
    for layer in range(depth):
        occ = layer // 2
        if layer % 2 == 0:
            dk = d // heads
            proj, rest = _matmul1(h, ret_w_in, occ, rope=(cos, sin, dk, float(dk) ** -0.5),
                                  slab=dk, side=(c_col, ada_w, ada_b, 2 * d))
            mods = [jnp.concatenate([mods[0], rest[:, :4 * d]], axis=1), rest[:, 4 * d:]]
            lg = jnp.stack([jax.nn.log_sigmoid(ret_decay_fwd[occ].astype(F32)),
                            jax.nn.log_sigmoid(ret_decay_bwd[occ].astype(F32))])
            yh = _retention(proj, lg, ret_gn_g[occ].reshape(1, 2 * d), heads)
            y = _matmul(yh, ret_w_out, occ, out_dtype=BF16, bm=2048, bk=2048)
        else:
            mixed = _fourier_real(h, groups)
            y = _matmul1(mixed, fno_w, occ, bias=fno_b[occ].reshape(1, d))
        x, h = _row_call(x, y=y, gate=mod(layer, 2), gy=gain(layer, 1),
                         gx=gain(layer, 2), sc=mod(layer, 4), sh=mod(layer, 3))
        a = _matmul1(h, mlp_w1, layer, act="relu2")
        y = _matmul(a, mlp_w2, layer, out_dtype=BF16, bm=2048, bk=2048)
        if layer + 1 < depth:
            x, h = _row_call(x, y=y, gate=mod(layer, 5), gy=gain(layer, 3),
                             gx=gain(layer + 1, 0), sc=mod(layer + 1, 1), sh=mod(layer + 1, 0))
        else:
            (x,) = _row_call(x, y=y, gate=mod(layer, 5), gy=gain(layer, 3))
    return x.reshape(batch, s, d)
```

```python
import functools

import numpy as np
import jax
import jax.numpy as jnp
from jax import lax
from jax.experimental import pallas as pl
from jax.experimental.pallas import tpu as pltpu

N_MOD = 6
RET_HEADS = 16
FNO_GROUPS = 8
ROPE_BASE = 10000.0
EPS = 1e-6

RET_CHUNK = 256
RET_BLOCK = 4096
DFT_N1 = 16
VMEM_LIMIT_BYTES = 60 * 1024 * 1024
ROW_SUB = 512
SIDE_COLS = 256
MMK_BLOCK = (2048, 1024, 2048)

F32 = jnp.float32
BF16 = jnp.bfloat16


def _params(*sem):
    return pltpu.CompilerParams(dimension_semantics=sem, vmem_limit_bytes=VMEM_LIMIT_BYTES)


def _adaln_kernel(c_ref, w_ref, b_ref, o_ref):
    c = c_ref[...]
    s = c * jax.nn.sigmoid(c)
    o_ref[0] = jnp.sum(w_ref[0] * s, axis=0, keepdims=True) + b_ref[0]


def _adaln(c_col, ada_w, ada_b, ncols):
    depth, d, n = ada_w.shape
    bn = min(512, ncols)
    return pl.pallas_call(
        _adaln_kernel,
        grid=(1, ncols // bn),
        in_specs=[pl.BlockSpec((d, 1), lambda l, j: (0, 0)),
                  pl.BlockSpec((1, d, bn), lambda l, j: (l, 0, j)),
                  pl.BlockSpec((1, 1, bn), lambda l, j: (l, 0, j))],
        out_specs=pl.BlockSpec((1, 1, bn), lambda l, j: (l, 0, j)),
        out_shape=jax.ShapeDtypeStruct((1, 1, ncols), F32),
        compiler_params=_params("parallel", "parallel"),
        name="adaln",
    )(c_col, ada_w, ada_b.reshape(depth, 1, n))


def _rms(x, g):
    ms = jnp.mean(x * x, axis=-1, keepdims=True)
    return x * lax.rsqrt(ms + EPS) * g


def _row_kernel(*refs, has_y, has_h):
    refs = list(refs)
    x_ref = refs.pop(0)
    x = x_ref[...]
    if has_y:
        y_ref, gate_ref, gy_ref = refs.pop(0), refs.pop(0), refs.pop(0)
    if has_h:
        gx_ref, sc_ref, sh_ref = refs.pop(0), refs.pop(0), refs.pop(0)
    if has_y:
        xo_ref = refs.pop(0)
        x = x + gate_ref[...] * _rms(y_ref[...].astype(F32), gy_ref[...])
        xo_ref[...] = x
    if has_h:
        h_ref = refs.pop(0)
        h_ref[...] = (_rms(x, gx_ref[...]) * (1.0 + sc_ref[...]) + sh_ref[...]).astype(BF16)


def _row_call(x, y=None, gate=None, gy=None, gx=None, sc=None, sh=None):
    s, d = x.shape
    bm = min(512, s)
    has_y, has_h = y is not None, gx is not None
    row = pl.BlockSpec((bm, d), lambda i: (i, 0))
    vec = pl.BlockSpec((1, d), lambda i: (0, 0))
    args, in_specs, out_specs, out_shape = [x], [row], [], []
    if has_y:
        args += [y, gate, gy]
        in_specs += [row, vec, vec]
        out_specs.append(row)
        out_shape.append(jax.ShapeDtypeStruct((s, d), F32))
    if has_h:
        args += [gx, sc, sh]
        in_specs += [vec, vec, vec]
        out_specs.append(row)
        out_shape.append(jax.ShapeDtypeStruct((s, d), BF16))
    out = pl.pallas_call(
        functools.partial(_row_kernel, has_y=has_y, has_h=has_h),
        grid=(s // bm,),
        in_specs=in_specs, out_specs=out_specs, out_shape=out_shape,
        compiler_params=_params("parallel"),
        name="row_norm",
    )(*args)
    return out


def _mmk_kernel(a_ref, w_ref, o_ref, wb_ref, acc_ref, *, nk, ck, bm):
    b, m = pl.program_id(0), pl.program_id(1)
    k = (b - 1) % nk
    rb = min(ROW_SUB, bm)

    def cast_next():
        wb_ref[b % 2, pl.ds(pl.multiple_of(m * ck, ck), ck), :] = w_ref[...].astype(BF16)

    def body(first, last):
        cast_next()
        wb = wb_ref.at[(b + 1) % 2]
        for r in range(bm // rb):
            part = jnp.dot(a_ref[r * rb:(r + 1) * rb, :], wb[...], preferred_element_type=F32)
            rows = pl.ds(pl.multiple_of(m * bm + r * rb, rb), rb)
            if not first:
                part = acc_ref[rows, :] + part
            if last:
                o_ref[r * rb:(r + 1) * rb, :] = part.astype(o_ref.dtype)
            else:
                acc_ref[rows, :] = part

    @pl.when(b == 0)
    def _():
        cast_next()

    @pl.when(jnp.logical_and(b > 0, k == 0))
    def _():
        body(True, False)

    @pl.when(jnp.logical_and(b > 0, jnp.logical_and(k > 0, k < nk - 1)))
    def _():
        body(False, False)

    @pl.when(jnp.logical_and(b > 0, k == nk - 1))
    def _():
        body(False, True)


def _matmulk(a, w, layer, halves=2):
    m, kd = a.shape
    _, _, n = w.shape
    bm, bn, bk = MMK_BLOCK
    nb, nk = n // bn, kd // bk
    mb = m // halves // bm
    ck = bk // mb
    nblocks = halves * nb * nk
    assert nk > 1 and mb * bm * halves == m and ck * mb == bk

    def dec(x):
        return x // (nk * nb), (x // nk) % nb, x % nk

    def a_map(b, j):
        hh, _, kk = dec(jnp.maximum(b - 1, 0))
        return hh * mb + jnp.where(b > 0, j, 0), kk

    def w_map(b, j):
        _, nn, kk = dec(jnp.minimum(b, nblocks - 1))
        return layer, kk * mb + j, nn

    def o_map(b, j):
        hh, nn, kk = dec(jnp.maximum(b - 1, 0))
        return hh * mb + jnp.where(jnp.logical_and(b > 0, kk == nk - 1), j, 0), nn

    return pl.pallas_call(
        functools.partial(_mmk_kernel, nk=nk, ck=ck, bm=bm),
        grid=(nblocks + 1, mb),
        in_specs=[pl.BlockSpec((bm, bk), a_map),
                  pl.BlockSpec((None, ck, bn), w_map)],
        out_specs=pl.BlockSpec((bm, bn), o_map),
        out_shape=jax.ShapeDtypeStruct((m, n), BF16),
        scratch_shapes=[pltpu.VMEM((2, bk, bn), BF16), pltpu.VMEM((m // halves, bn), F32)],
        compiler_params=_params("arbitrary", "arbitrary"),
        name="matmulk",
    )(a, w)


def _mm1_kernel(*refs, nb, ck, act, has_bias, rope, side):
    refs = list(refs)
    a_ref, w_ref = refs.pop(0), refs.pop(0)
    b_ref = refs.pop(0) if has_bias else None
    cos_ref, sin_ref = (refs.pop(0), refs.pop(0)) if rope else (None, None)
    if side:
        c_ref, aw_ref, ab_ref = refs.pop(0), refs.pop(0), refs.pop(0)
        o_ref, mo_ref, wb_ref, sb_ref = refs
    else:
        o_ref, wb_ref = refs
    n, m = pl.program_id(0), pl.program_id(1)

    def cast_next():
        wb_ref[n % 2, pl.ds(pl.multiple_of(m * ck, ck), ck), :] = w_ref[...].astype(BF16)

    def epilogue(acc):
        if has_bias:
            acc = acc + b_ref[...]
        if act == "relu2":
            acc = jnp.maximum(acc, 0.0)
            acc = acc * acc
        return acc

    def compute(rotary):
        cast_next()
        if side:
            lanes = sb_ref.shape[1]
            mv = [jnp.sum(aw_ref[:, t * lanes:(t + 1) * lanes] * sb_ref[...], axis=0, keepdims=True)
                  for t in range(aw_ref.shape[1] // lanes)]
            mo_ref[...] = jnp.concatenate(mv, axis=-1) + ab_ref[...]
        bm = a_ref.shape[0]
        rb = min(ROW_SUB, bm)
        wb = wb_ref.at[(n + 1) % 2]
        if rotary:
            nq, dk, k_scale = rope
            scale = jnp.where(n - 1 >= nq, k_scale, 1.0)
        for r in range(bm // rb):
            rows = slice(r * rb, (r + 1) * rb)
            acc = epilogue(jnp.dot(a_ref[rows, :], wb[...], preferred_element_type=F32))
            if rotary:
                c, s = cos_ref[rows, :] * scale, sin_ref[rows, :] * scale
                half = dk // 2
                parts = []
                for hh in range(acc.shape[1] // dk):
                    x1 = acc[:, hh * dk:hh * dk + half]
                    x2 = acc[:, hh * dk + half:(hh + 1) * dk]
                    parts += [x1 * c - x2 * s, x1 * s + x2 * c]
                acc = jnp.concatenate(parts, axis=-1)
            if len(o_ref.shape) == 3:
                slab = o_ref.shape[2]
                for hh in range(o_ref.shape[0]):
                    o_ref[hh, rows, :] = acc[:, hh * slab:(hh + 1) * slab].astype(o_ref.dtype)
            else:
                o_ref[rows, :] = acc.astype(o_ref.dtype)

    @pl.when(n == 0)
    def _():
        cast_next()

    if side:
        @pl.when(jnp.logical_and(n == 0, m == 0))
        def _():
            c = c_ref[...]
            sb_ref[...] = jnp.broadcast_to(c * jax.nn.sigmoid(c), sb_ref.shape)

    if rope:
        nqk = 2 * rope[0]

        @pl.when(jnp.logical_and(n > 0, n - 1 < nqk))
        def _():
            compute(True)

        @pl.when(n - 1 >= nqk)
        def _():
            compute(False)
    else:
        @pl.when(n > 0)
        def _():
            compute(False)


def _matmul1(a, w, layer, bias=None, act=None, out_dtype=BF16, bm=1024, bn=1024, rope=None,
             slab=None, side=None):
    m, kd = a.shape
    _, _, n = w.shape
    bm, bn = min(bm, m), min(bn, n)
    nb, mb = n // bn, m // bm
    ck = kd // mb
    has_bias = bias is not None

    def row(i, j):
        return jnp.where(i > 0, j, 0)

    def col(i):
        return jnp.maximum(i - 1, 0)

    in_specs = [pl.BlockSpec((bm, kd), lambda i, j: (row(i, j), 0)),
                pl.BlockSpec((None, ck, bn), lambda i, j: (layer, j, jnp.minimum(i, nb - 1)))]
    args = [a, w]
    if has_bias:
        in_specs.append(pl.BlockSpec((1, bn), lambda i, j: (0, col(i))))
        args.append(bias)
    rope_static = None
    if rope is not None:
        cos, sin, dk, k_scale = rope
        in_specs += [pl.BlockSpec((bm, dk // 2), lambda i, j: (row(i, j), 0))] * 2
        args += [cos, sin]
        rope_static = (dk * (n // 6 // dk) // bn, dk, k_scale)
    if slab is None:
        out_specs = [pl.BlockSpec((bm, bn), lambda i, j: (row(i, j), col(i)))]
        out_shape = [jax.ShapeDtypeStruct((m, n), out_dtype)]
    else:
        out_specs = [pl.BlockSpec((bn // slab, bm, slab), lambda i, j: (col(i), row(i, j), 0))]
        out_shape = [jax.ShapeDtypeStruct((n // slab, m, slab), out_dtype)]
    scratch = [pltpu.VMEM((2, kd, bn), BF16)]
    if side is not None:
        c_col, ada_w, ada_b, first_col = side
        layers, da, na = ada_w.shape
        todo = layers * na - first_col
        sw = SIDE_COLS
        nside = todo // sw
        assert nside * sw == todo and first_col % sw == 0 and na % sw == 0 and nside <= nb * mb

        def sblk(i, j):
            return jnp.minimum(col(i) * mb + row(i, j), nside - 1)

        def flat(i, j):
            return first_col // sw + sblk(i, j)

        per_layer = na // sw
        in_specs += [pl.BlockSpec((da, 1), lambda i, j: (0, 0)),
                     pl.BlockSpec((None, da, sw), lambda i, j: (flat(i, j) // per_layer, 0, flat(i, j) % per_layer)),
                     pl.BlockSpec((None, 1, sw), lambda i, j: (flat(i, j) // per_layer, 0, flat(i, j) % per_layer))]
        args += [c_col, ada_w, ada_b.reshape(layers, 1, na)]
        out_specs.append(pl.BlockSpec((1, sw), lambda i, j: (0, sblk(i, j))))
        out_shape.append(jax.ShapeDtypeStruct((1, todo), F32))
        scratch.append(pltpu.VMEM((da, 128), F32))
    out = pl.pallas_call(
        functools.partial(_mm1_kernel, nb=nb, ck=ck, act=act, has_bias=has_bias, rope=rope_static,
                          side=side is not None),
        grid=(nb + 1, mb),
        in_specs=in_specs,
        out_specs=out_specs,
        out_shape=out_shape,
        scratch_shapes=scratch,
        compiler_params=_params("arbitrary", "arbitrary"),
        name="matmul1",
    )(*args)
    return out if side is not None else out[0]


def _ret_kernel(lg_ref, q_ref, k_ref, v_ref, g_ref, gn_ref,
                o_ref, yacc_ref, st_ref, *, chunk, nsub):
    h, p, i = pl.program_id(0), pl.program_id(1), pl.program_id(2)
    nblk = pl.num_programs(2)
    blk_rows = chunk * nsub
    dk = q_ref.shape[1]
    lgf, lgb = lg_ref[0, h], lg_ref[1, h]

    @pl.when(i == 0)
    def _():
        st_ref[...] = jnp.zeros_like(st_ref)

    r = lax.broadcasted_iota(jnp.int32, (chunk, 1), 0).astype(F32)

    def wide(ref, rows):
        return jnp.concatenate([ref[0, rows, :], ref[1, rows, :]], axis=-1)

    def decay(expo):
        return jnp.broadcast_to(jnp.exp(expo), (chunk, dk)).astype(BF16)

    def state_update(k, kdec, v, lg):
        upd = lax.dot_general(k * kdec, v, (((0,), (0,)), ((), ())), preferred_element_type=F32)
        st_ref[...] = st_ref[...] * jnp.exp(jnp.full((1, 1), chunk * lg, F32)) + upd

    @pl.when(p == 0)
    def _():
        ri = lax.broadcasted_iota(jnp.int32, (chunk, chunk), 0)
        ci = lax.broadcasted_iota(jnp.int32, (chunk, chunk), 1)
        diff = (ri - ci).astype(F32)
        dmat = (jnp.where(diff >= 0, jnp.exp(jnp.maximum(diff, 0.0) * lgf), 0.0)
                + jnp.where(diff <= 0, jnp.exp(jnp.maximum(-diff, 0.0) * lgb), 0.0))
        qdec = decay((r + 1.0) * lgf)
        kdec = decay((chunk - 1.0 - r) * lgf)
        for s in range(nsub):
            rows = pl.ds(s * chunk, chunk)
            q, k, v = q_ref[rows, :], k_ref[rows, :], wide(v_ref, rows)
            sc = lax.dot_general(q, k, (((1,), (1,)), ((), ())), preferred_element_type=F32)
            intra = jnp.dot((sc * dmat).astype(BF16), v, preferred_element_type=F32)
            cross = jnp.dot(q * qdec, st_ref[...].astype(BF16), preferred_element_type=F32)
            yacc_ref[pl.ds(pl.multiple_of(i * blk_rows + s * chunk, chunk), chunk), :] = intra + cross
            state_update(k, kdec, v, lgf)

    @pl.when(p == 1)
    def _():
        qdec = decay((chunk - r) * lgb)
        kdec = decay(r * lgb)
        base = (nblk - 1 - i) * blk_rows
        for s in reversed(range(nsub)):
            rows = pl.ds(s * chunk, chunk)
            q, k, v = q_ref[rows, :], k_ref[rows, :], wide(v_ref, rows)
            cross = jnp.dot(q * qdec, st_ref[...].astype(BF16), preferred_element_type=F32)
            y = yacc_ref[pl.ds(pl.multiple_of(base + s * chunk, chunk), chunk), :] + cross
            mu = jnp.mean(y, axis=-1, keepdims=True)
            yc = y - mu
            var = jnp.mean(yc * yc, axis=-1, keepdims=True)
            yn = yc * lax.rsqrt(var + EPS) * gn_ref[...]
            g = wide(g_ref, rows).astype(F32)
            o_ref[rows, :] = (g * jax.nn.sigmoid(g) * yn).astype(o_ref.dtype)
            state_update(k, kdec, v, lgb)


def _retention(proj, lg, gn_g, heads):
    nslab, s, dk = proj.shape
    d = nslab * dk // 6
    dv = 2 * dk
    chunk = min(RET_CHUNK, s)
    blk = min(RET_BLOCK, s)
    nblk = s // blk

    def cidx(p, i):
        return i + p * (nblk - 1 - 2 * i)

    grid_spec = pltpu.PrefetchScalarGridSpec(
        num_scalar_prefetch=1,
        grid=(heads, 2, nblk),
        in_specs=[
            pl.BlockSpec((None, blk, dk), lambda h, p, i, lg: (h, cidx(p, i), 0)),
            pl.BlockSpec((None, blk, dk), lambda h, p, i, lg: (heads + h, cidx(p, i), 0)),
            pl.BlockSpec((2, blk, dk), lambda h, p, i, lg: (heads + h, cidx(p, i), 0)),
            pl.BlockSpec((2, blk, dk), lambda h, p, i, lg: (2 * heads + h, nblk - 1 - i * p, 0)),
            pl.BlockSpec((1, dv), lambda h, p, i, lg: (0, h)),
        ],
        out_specs=pl.BlockSpec((blk, dv), lambda h, p, i, lg: (nblk - 1 - i * p, h)),
        scratch_shapes=[pltpu.VMEM((s, dv), F32), pltpu.VMEM((dk, dv), F32)],
    )
    return pl.pallas_call(
        functools.partial(_ret_kernel, chunk=chunk, nsub=blk // chunk),
        grid_spec=grid_spec,
        out_shape=jax.ShapeDtypeStruct((s, 2 * d), BF16),
        compiler_params=_params("arbitrary", "arbitrary", "arbitrary"),
        name="retention",
    )(lg, proj, proj, proj, proj, gn_g)


def _dft_tables(n):
    j = np.arange(n)
    ang = 2.0 * np.pi * ((j[:, None] * j[None, :]) % n) / n
    return np.cos(ang), np.sin(ang)


def _fourier_stage1_kernel(h_ref, cs_ref, ma_ref, mb_ref, wr_ref, wi_ref, o_ref, *, tile):
    n1, jb, dg = h_ref.shape
    r = jnp.dot(h_ref[...].reshape(n1 * jb, dg), cs_ref[...], preferred_element_type=F32)
    r = r.astype(BF16).reshape(n1, jb, 2 * dg)
    half = n1 * tile
    for s in range(jb // tile):
        rs = r[:, s * tile:(s + 1) * tile, :].reshape(half, 2 * dg)
        t = (jnp.dot(ma_ref[...], rs[:, :dg], preferred_element_type=F32)
             + jnp.dot(mb_ref[...], rs[:, dg:], preferred_element_type=F32))
        tr, ti = t[:half], t[half:]
        wr, wi = wr_ref[s], wi_ref[s]
        rows = pl.ds(s * tile, tile)
        o_ref[0, :, rows, :] = (tr * wr - ti * wi).astype(o_ref.dtype).reshape(n1, tile, dg)
        o_ref[1, :, rows, :] = (tr * wi + ti * wr).astype(o_ref.dtype).reshape(n1, tile, dg)


def _fourier_stage2_kernel(t_ref, c_ref, s_ref, o_ref):
    n1 = t_ref.shape[1]
    res = [jnp.dot(c_ref[...], t_ref[0, kk], preferred_element_type=F32)
           + jnp.dot(s_ref[...], t_ref[1, kk], preferred_element_type=F32) for kk in range(n1)]
    o_ref[...] = jnp.swapaxes(jnp.stack(res), 0, 1).astype(o_ref.dtype)


def _fourier_real(h, groups):
    s, d = h.shape
    dg = d // groups
    n1 = DFT_N1
    n2 = s // n1
    tile = 16
    jb = min(64, n2)

    cd, sd = _dft_tables(dg)
    cs = jnp.asarray(np.concatenate([cd, sd], axis=1), F32).astype(BF16)
    c1, s1 = _dft_tables(n1)
    eye = np.eye(tile)
    ma = jnp.asarray(np.kron(np.concatenate([c1, -s1], axis=0), eye), F32).astype(BF16)
    mb = jnp.asarray(np.kron(np.concatenate([-s1, -c1], axis=0), eye), F32).astype(BF16)
    tw = 2.0 * np.pi * (np.arange(n1)[None, :, None]
                        * (np.arange(n2 // tile)[:, None, None] * tile + np.arange(tile)[None, None, :])) / s
    tw = tw.reshape(n2 // tile, n1 * tile, 1)
    wr = jnp.asarray(np.cos(tw), F32)
    wi = jnp.asarray(-np.sin(tw), F32)
    tp = pl.pallas_call(
        functools.partial(_fourier_stage1_kernel, tile=tile),
        grid=(n2 // jb, groups),
        in_specs=[pl.BlockSpec((n1, jb, dg), lambda j, g: (0, j, g)),
                  pl.BlockSpec((dg, 2 * dg), lambda j, g: (0, 0)),
                  pl.BlockSpec((2 * n1 * tile, n1 * tile), lambda j, g: (0, 0)),
                  pl.BlockSpec((2 * n1 * tile, n1 * tile), lambda j, g: (0, 0)),
                  pl.BlockSpec((jb // tile, n1 * tile, 1), lambda j, g: (j, 0, 0)),
                  pl.BlockSpec((jb // tile, n1 * tile, 1), lambda j, g: (j, 0, 0))],
        out_specs=pl.BlockSpec((2, n1, jb, dg), lambda j, g: (0, 0, j, g)),
        out_shape=jax.ShapeDtypeStruct((2, n1, n2, d), BF16),
        compiler_params=_params("parallel", "parallel"),
        name="fourier_stage1",
    )(h.reshape(n1, n2, d), cs, ma, mb, wr, wi)

    c2, s2 = _dft_tables(n2)
    db = min(256, d)
    mixed = pl.pallas_call(
        _fourier_stage2_kernel,
        grid=(d // db,),
        in_specs=[pl.BlockSpec((2, n1, n2, db), lambda e: (0, 0, 0, e)),
                  pl.BlockSpec((n2, n2), lambda e: (0, 0)),
                  pl.BlockSpec((n2, n2), lambda e: (0, 0))],
        out_specs=pl.BlockSpec((n2, n1, db), lambda e: (0, 0, e)),
        out_shape=jax.ShapeDtypeStruct((n2, n1, d), BF16),
        compiler_params=_params("parallel"),
        name="fourier_stage2",
    )(tp, jnp.asarray(c2, F32).astype(BF16), jnp.asarray(s2, F32).astype(BF16))
    return mixed.reshape(s, d)


def kernel(x, c, ada_w, ada_b, norm_g, ret_w_in, ret_w_out, ret_gn_g, ret_decay_fwd,
           ret_decay_bwd, fno_w, fno_b, mlp_w1, mlp_w2):
    batch, s, d = x.shape
    assert batch == 1
    depth = ada_w.shape[0]
    heads, groups = RET_HEADS, FNO_GROUPS
    x = x.reshape(s, d)

    assert depth == 2
    c_col = c.reshape(d, 1)
    mods = [_adaln(c_col, ada_w, ada_b, 2 * d)[0], None]

    def mod(layer, idx):
        return mods[layer][:, idx * d:(idx + 1) * d]

    def gain(layer, idx):
        return norm_g[layer, idx].reshape(1, d)

    half = d // heads // 2
    inv = ROPE_BASE ** (-jnp.arange(half, dtype=F32) / half)
    ang = jnp.arange(s, dtype=F32)[:, None] * inv[None, :]
    cos, sin = jnp.cos(ang), jnp.sin(ang)

    (h,) = _row_call(x, gx=gain(0, 0), sc=mod(0, 1), sh=mod(0, 0))
    for layer in range(depth):
        occ = layer // 2
        if layer % 2 == 0:
            dk = d // heads
            proj, rest = _matmul1(h, ret_w_in, occ, rope=(cos, sin, dk, float(dk) ** -0.5),
                                  slab=dk, side=(c_col, ada_w, ada_b, 2 * d))
            mods = [jnp.concatenate([mods[0], rest[:, :4 * d]], axis=1), rest[:, 4 * d:]]
            lg = jnp.stack([jax.nn.log_sigmoid(ret_decay_fwd[occ].astype(F32)),
                            jax.nn.log_sigmoid(ret_decay_bwd[occ].astype(F32))])
            yh = _retention(proj, lg, ret_gn_g[occ].reshape(1, 2 * d), heads)
            y = _matmulk(yh, ret_w_out, occ)
        else:
            mixed = _fourier_real(h, groups)
            y = _matmul1(mixed, fno_w, occ, bias=fno_b[occ].reshape(1, d))
        x, h = _row_call(x, y=y, gate=mod(layer, 2), gy=gain(layer, 1),
                         gx=gain(layer, 2), sc=mod(layer, 4), sh=mod(layer, 3))
        a = _matmul1(h, mlp_w1, layer, act="relu2")
        y = _matmulk(a, mlp_w2, layer)
        if layer + 1 < depth:
            x, h = _row_call(x, y=y, gate=mod(layer, 5), gy=gain(layer, 3),
                             gx=gain(layer + 1, 0), sc=mod(layer + 1, 1), sh=mod(layer + 1, 0))
        else:
            (x,) = _row_call(x, y=y, gate=mod(layer, 5), gy=gain(layer, 3))
    return x.reshape(batch, s, d)
```

```python
import functools

import numpy as np
import jax
import jax.numpy as jnp
from jax import lax
from jax.experimental import pallas as pl
from jax.experimental.pallas import tpu as pltpu

N_MOD = 6
RET_HEADS = 16
FNO_GROUPS = 8
ROPE_BASE = 10000.0
EPS = 1e-6

V7X_VMEM_BYTES = 64 * 1024 * 1024
V7X_LANES = 128
BF16_SUBLANE_TILE = 16
VMEM_LIMIT_BYTES = V7X_VMEM_BYTES - 4 * 1024 * 1024

RET_CHUNK = 256
RET_BLOCK = 4096
DFT_N1 = 16
DFT_STEP_ROWS = 128
DFT_STEP_COLS = 256
ROW_BLOCK = 512
ADALN_COLS = 512
MM1_BLOCK = (1024, 1024)
MMK_BLOCK = (2048, 1024, 2048)
MM1_ROW_SUB = 512
MMK_ROW_SUB = 512
SIDE_COLS = 256

F32 = jnp.float32
BF16 = jnp.bfloat16


def _params(*sem):
    return pltpu.CompilerParams(dimension_semantics=sem, vmem_limit_bytes=VMEM_LIMIT_BYTES)


def _adaln_kernel(c_ref, w_ref, b_ref, o_ref):
    c = c_ref[...]
    s = c * jax.nn.sigmoid(c)
    o_ref[0] = jnp.sum(w_ref[0] * s, axis=0, keepdims=True) + b_ref[0]


def _adaln(c_col, ada_w, ada_b, ncols):
    depth, d, n = ada_w.shape
    bn = min(ADALN_COLS, ncols)
    return pl.pallas_call(
        _adaln_kernel,
        grid=(1, ncols // bn),
        in_specs=[pl.BlockSpec((d, 1), lambda l, j: (0, 0)),
                  pl.BlockSpec((1, d, bn), lambda l, j: (l, 0, j)),
                  pl.BlockSpec((1, 1, bn), lambda l, j: (l, 0, j))],
        out_specs=pl.BlockSpec((1, 1, bn), lambda l, j: (l, 0, j)),
        out_shape=jax.ShapeDtypeStruct((1, 1, ncols), F32),
        compiler_params=_params("parallel", "parallel"),
        name="adaln",
    )(c_col, ada_w, ada_b.reshape(depth, 1, n))


def _rms(x, g):
    ms = jnp.mean(x * x, axis=-1, keepdims=True)
    return x * lax.rsqrt(ms + EPS) * g


def _row_kernel(*refs, has_y, has_h):
    refs = list(refs)
    x_ref = refs.pop(0)
    x = x_ref[...]
    if has_y:
        y_ref, gate_ref, gy_ref = refs.pop(0), refs.pop(0), refs.pop(0)
    if has_h:
        gx_ref, sc_ref, sh_ref = refs.pop(0), refs.pop(0), refs.pop(0)
    if has_y:
        xo_ref = refs.pop(0)
        x = x + gate_ref[...] * _rms(y_ref[...].astype(F32), gy_ref[...])
        xo_ref[...] = x
    if has_h:
        h_ref = refs.pop(0)
        h_ref[...] = (_rms(x, gx_ref[...]) * (1.0 + sc_ref[...]) + sh_ref[...]).astype(BF16)


def _row_call(x, y=None, gate=None, gy=None, gx=None, sc=None, sh=None):
    s, d = x.shape
    bm = min(ROW_BLOCK, s)
    has_y, has_h = y is not None, gx is not None
    row = pl.BlockSpec((bm, d), lambda i: (i, 0))
    vec = pl.BlockSpec((1, d), lambda i: (0, 0))
    args, in_specs, out_specs, out_shape = [x], [row], [], []
    if has_y:
        args += [y, gate, gy]
        in_specs += [row, vec, vec]
        out_specs.append(row)
        out_shape.append(jax.ShapeDtypeStruct((s, d), F32))
    if has_h:
        args += [gx, sc, sh]
        in_specs += [vec, vec, vec]
        out_specs.append(row)
        out_shape.append(jax.ShapeDtypeStruct((s, d), BF16))
    out = pl.pallas_call(
        functools.partial(_row_kernel, has_y=has_y, has_h=has_h),
        grid=(s // bm,),
        in_specs=in_specs, out_specs=out_specs, out_shape=out_shape,
        compiler_params=_params("parallel"),
        name="row_norm",
    )(*args)
    return out


def _mmk_kernel(a_ref, w_ref, o_ref, wb_ref, acc_ref, *, nk, ck, bm):
    b, m = pl.program_id(0), pl.program_id(1)
    k = (b - 1) % nk
    rb = min(MMK_ROW_SUB, bm)

    def cast_next():
        wb_ref[b % 2, pl.ds(pl.multiple_of(m * ck, ck), ck), :] = w_ref[...].astype(BF16)

    def body(first, last):
        cast_next()
        wb = wb_ref.at[(b + 1) % 2]
        for r in range(bm // rb):
            part = jnp.dot(a_ref[r * rb:(r + 1) * rb, :], wb[...], preferred_element_type=F32)
            rows = pl.ds(pl.multiple_of(m * bm + r * rb, rb), rb)
            if not first:
                part = acc_ref[rows, :] + part
            if last:
                o_ref[r * rb:(r + 1) * rb, :] = part.astype(o_ref.dtype)
            else:
                acc_ref[rows, :] = part

    @pl.when(b == 0)
    def _():
        cast_next()

    @pl.when(jnp.logical_and(b > 0, k == 0))
    def _():
        body(True, False)

    @pl.when(jnp.logical_and(b > 0, jnp.logical_and(k > 0, k < nk - 1)))
    def _():
        body(False, False)

    @pl.when(jnp.logical_and(b > 0, k == nk - 1))
    def _():
        body(False, True)


def _matmulk(a, w, layer, halves=2):
    m, kd = a.shape
    _, _, n = w.shape
    bm, bn, bk = MMK_BLOCK
    nb, nk = n // bn, kd // bk
    mb = m // halves // bm
    ck = bk // mb
    nblocks = halves * nb * nk
    assert nk > 1 and mb * bm * halves == m and ck * mb == bk

    def dec(x):
        return x // (nk * nb), (x // nk) % nb, x % nk

    def a_map(b, j):
        hh, _, kk = dec(jnp.maximum(b - 1, 0))
        return hh * mb + jnp.where(b > 0, j, 0), kk

    def w_map(b, j):
        _, nn, kk = dec(jnp.minimum(b, nblocks - 1))
        return layer, kk * mb + j, nn

    def o_map(b, j):
        hh, nn, kk = dec(jnp.maximum(b - 1, 0))
        return hh * mb + jnp.where(jnp.logical_and(b > 0, kk == nk - 1), j, 0), nn

    return pl.pallas_call(
        functools.partial(_mmk_kernel, nk=nk, ck=ck, bm=bm),
        grid=(nblocks + 1, mb),
        in_specs=[pl.BlockSpec((bm, bk), a_map),
                  pl.BlockSpec((None, ck, bn), w_map)],
        out_specs=pl.BlockSpec((bm, bn), o_map),
        out_shape=jax.ShapeDtypeStruct((m, n), BF16),
        scratch_shapes=[pltpu.VMEM((2, bk, bn), BF16), pltpu.VMEM((m // halves, bn), F32)],
        compiler_params=_params("arbitrary", "arbitrary"),
        name="matmulk",
    )(a, w)


def _mm1_kernel(*refs, nb, ck, act, has_bias, rope, side):
    refs = list(refs)
    a_ref, w_ref = refs.pop(0), refs.pop(0)
    b_ref = refs.pop(0) if has_bias else None
    cos_ref, sin_ref = (refs.pop(0), refs.pop(0)) if rope else (None, None)
    if side:
        c_ref, aw_ref, ab_ref = refs.pop(0), refs.pop(0), refs.pop(0)
        o_ref, mo_ref, wb_ref, sb_ref = refs
    else:
        o_ref, wb_ref = refs
    n, m = pl.program_id(0), pl.program_id(1)

    def cast_next():
        wb_ref[n % 2, pl.ds(pl.multiple_of(m * ck, ck), ck), :] = w_ref[...].astype(BF16)

    def epilogue(acc):
        if has_bias:
            acc = acc + b_ref[...]
        if act == "relu2":
            acc = jnp.maximum(acc, 0.0)
            acc = acc * acc
        return acc

    def compute(rotary):
        cast_next()
        if side:
            lanes = sb_ref.shape[1]
            mv = [jnp.sum(aw_ref[:, t * lanes:(t + 1) * lanes] * sb_ref[...], axis=0, keepdims=True)
                  for t in range(aw_ref.shape[1] // lanes)]
            mo_ref[...] = jnp.concatenate(mv, axis=-1) + ab_ref[...]
        bm = a_ref.shape[0]
        rb = min(MM1_ROW_SUB, bm)
        wb = wb_ref.at[(n + 1) % 2]
        if rotary:
            nq, dk, k_scale = rope
            scale = jnp.where(n - 1 >= nq, k_scale, 1.0)
        for r in range(bm // rb):
            rows = slice(r * rb, (r + 1) * rb)
            acc = epilogue(jnp.dot(a_ref[rows, :], wb[...], preferred_element_type=F32))
            if rotary:
                c, s = cos_ref[rows, :] * scale, sin_ref[rows, :] * scale
                half = dk // 2
                parts = []
                for hh in range(acc.shape[1] // dk):
                    x1 = acc[:, hh * dk:hh * dk + half]
                    x2 = acc[:, hh * dk + half:(hh + 1) * dk]
                    parts += [x1 * c - x2 * s, x1 * s + x2 * c]
                acc = jnp.concatenate(parts, axis=-1)
            if len(o_ref.shape) == 3:
                slab = o_ref.shape[2]
                for hh in range(o_ref.shape[0]):
                    o_ref[hh, rows, :] = acc[:, hh * slab:(hh + 1) * slab].astype(o_ref.dtype)
            else:
                o_ref[rows, :] = acc.astype(o_ref.dtype)

    @pl.when(n == 0)
    def _():
        cast_next()

    if side:
        @pl.when(jnp.logical_and(n == 0, m == 0))
        def _():
            c = c_ref[...]
            sb_ref[...] = jnp.broadcast_to(c * jax.nn.sigmoid(c), sb_ref.shape)

    if rope:
        nqk = 2 * rope[0]

        @pl.when(jnp.logical_and(n > 0, n - 1 < nqk))
        def _():
            compute(True)

        @pl.when(n - 1 >= nqk)
        def _():
            compute(False)
    else:
        @pl.when(n > 0)
        def _():
            compute(False)


def _matmul1(a, w, layer, bias=None, act=None, out_dtype=BF16, rope=None, slab=None, side=None):
    m, kd = a.shape
    _, _, n = w.shape
    bm, bn = min(MM1_BLOCK[0], m), min(MM1_BLOCK[1], n)
    nb, mb = n // bn, m // bm
    ck = kd // mb
    has_bias = bias is not None

    def row(i, j):
        return jnp.where(i > 0, j, 0)

    def col(i):
        return jnp.maximum(i - 1, 0)

    in_specs = [pl.BlockSpec((bm, kd), lambda i, j: (row(i, j), 0)),
                pl.BlockSpec((None, ck, bn), lambda i, j: (layer, j, jnp.minimum(i, nb - 1)))]
    args = [a, w]
    if has_bias:
        in_specs.append(pl.BlockSpec((1, bn), lambda i, j: (0, col(i))))
        args.append(bias)
    rope_static = None
    if rope is not None:
        cos, sin, dk, k_scale = rope
        in_specs += [pl.BlockSpec((bm, dk // 2), lambda i, j: (row(i, j), 0))] * 2
        args += [cos, sin]
        rope_static = (dk * (n // 6 // dk) // bn, dk, k_scale)
    if slab is None:
        out_specs = [pl.BlockSpec((bm, bn), lambda i, j: (row(i, j), col(i)))]
        out_shape = [jax.ShapeDtypeStruct((m, n), out_dtype)]
    else:
        out_specs = [pl.BlockSpec((bn // slab, bm, slab), lambda i, j: (col(i), row(i, j), 0))]
        out_shape = [jax.ShapeDtypeStruct((n // slab, m, slab), out_dtype)]
    scratch = [pltpu.VMEM((2, kd, bn), BF16)]
    if side is not None:
        c_col, ada_w, ada_b, first_col = side
        layers, da, na = ada_w.shape
        todo = layers * na - first_col
        sw = SIDE_COLS
        nside = todo // sw
        assert nside * sw == todo and first_col % sw == 0 and na % sw == 0 and nside <= nb * mb

        def sblk(i, j):
            return jnp.minimum(col(i) * mb + row(i, j), nside - 1)

        def flat(i, j):
            return first_col // sw + sblk(i, j)

        per_layer = na // sw
        in_specs += [pl.BlockSpec((da, 1), lambda i, j: (0, 0)),
                     pl.BlockSpec((None, da, sw), lambda i, j: (flat(i, j) // per_layer, 0, flat(i, j) % per_layer)),
                     pl.BlockSpec((None, 1, sw), lambda i, j: (flat(i, j) // per_layer, 0, flat(i, j) % per_layer))]
        args += [c_col, ada_w, ada_b.reshape(layers, 1, na)]
        out_specs.append(pl.BlockSpec((1, sw), lambda i, j: (0, sblk(i, j))))
        out_shape.append(jax.ShapeDtypeStruct((1, todo), F32))
        scratch.append(pltpu.VMEM((da, V7X_LANES), F32))
    out = pl.pallas_call(
        functools.partial(_mm1_kernel, nb=nb, ck=ck, act=act, has_bias=has_bias, rope=rope_static,
                          side=side is not None),
        grid=(nb + 1, mb),
        in_specs=in_specs,
        out_specs=out_specs,
        out_shape=out_shape,
        scratch_shapes=scratch,
        compiler_params=_params("arbitrary", "arbitrary"),
        name="matmul1",
    )(*args)
    return out if side is not None else out[0]


def _ret_kernel(lg_ref, q_ref, k_ref, v_ref, g_ref, gn_ref,
                o_ref, yacc_ref, st_ref, *, chunk, nsub):
    h, p, i = pl.program_id(0), pl.program_id(1), pl.program_id(2)
    nblk = pl.num_programs(2)
    blk_rows = chunk * nsub
    dk = q_ref.shape[1]
    lgf, lgb = lg_ref[0, h], lg_ref[1, h]

    @pl.when(i == 0)
    def _():
        st_ref[...] = jnp.zeros_like(st_ref)

    r = lax.broadcasted_iota(jnp.int32, (chunk, 1), 0).astype(F32)

    def wide(ref, rows):
        return jnp.concatenate([ref[0, rows, :], ref[1, rows, :]], axis=-1)

    def decay(expo):
        return jnp.broadcast_to(jnp.exp(expo), (chunk, dk)).astype(BF16)

    def state_update(k, kdec, v, lg):
        upd = lax.dot_general(k * kdec, v, (((0,), (0,)), ((), ())), preferred_element_type=F32)
        st_ref[...] = st_ref[...] * jnp.exp(jnp.full((1, 1), chunk * lg, F32)) + upd

    @pl.when(p == 0)
    def _():
        ri = lax.broadcasted_iota(jnp.int32, (chunk, chunk), 0)
        ci = lax.broadcasted_iota(jnp.int32, (chunk, chunk), 1)
        diff = (ri - ci).astype(F32)
        dmat = (jnp.where(diff >= 0, jnp.exp(jnp.maximum(diff, 0.0) * lgf), 0.0)
                + jnp.where(diff <= 0, jnp.exp(jnp.maximum(-diff, 0.0) * lgb), 0.0))
        qdec = decay((r + 1.0) * lgf)
        kdec = decay((chunk - 1.0 - r) * lgf)
        for s in range(nsub):
            rows = pl.ds(s * chunk, chunk)
            q, k, v = q_ref[rows, :], k_ref[rows, :], wide(v_ref, rows)
            sc = lax.dot_general(q, k, (((1,), (1,)), ((), ())), preferred_element_type=F32)
            intra = jnp.dot((sc * dmat).astype(BF16), v, preferred_element_type=F32)
            cross = jnp.dot(q * qdec, st_ref[...].astype(BF16), preferred_element_type=F32)
            yacc_ref[pl.ds(pl.multiple_of(i * blk_rows + s * chunk, chunk), chunk), :] = intra + cross
            state_update(k, kdec, v, lgf)

    @pl.when(p == 1)
    def _():
        qdec = decay((chunk - r) * lgb)
        kdec = decay(r * lgb)
        base = (nblk - 1 - i) * blk_rows
        for s in reversed(range(nsub)):
            rows = pl.ds(s * chunk, chunk)
            q, k, v = q_ref[rows, :], k_ref[rows, :], wide(v_ref, rows)
            cross = jnp.dot(q * qdec, st_ref[...].astype(BF16), preferred_element_type=F32)
            y = yacc_ref[pl.ds(pl.multiple_of(base + s * chunk, chunk), chunk), :] + cross
            mu = jnp.mean(y, axis=-1, keepdims=True)
            yc = y - mu
            var = jnp.mean(yc * yc, axis=-1, keepdims=True)
            yn = yc * lax.rsqrt(var + EPS) * gn_ref[...]
            g = wide(g_ref, rows).astype(F32)
            o_ref[rows, :] = (g * jax.nn.sigmoid(g) * yn).astype(o_ref.dtype)
            state_update(k, kdec, v, lgb)


def _retention(proj, lg, gn_g, heads):
    nslab, s, dk = proj.shape
    d = nslab * dk // 6
    dv = 2 * dk
    chunk = min(RET_CHUNK, s)
    blk = min(RET_BLOCK, s)
    nblk = s // blk

    def cidx(p, i):
        return i + p * (nblk - 1 - 2 * i)

    grid_spec = pltpu.PrefetchScalarGridSpec(
        num_scalar_prefetch=1,
        grid=(heads, 2, nblk),
        in_specs=[
            pl.BlockSpec((None, blk, dk), lambda h, p, i, lg: (h, cidx(p, i), 0)),
            pl.BlockSpec((None, blk, dk), lambda h, p, i, lg: (heads + h, cidx(p, i), 0)),
            pl.BlockSpec((2, blk, dk), lambda h, p, i, lg: (heads + h, cidx(p, i), 0)),
            pl.BlockSpec((2, blk, dk), lambda h, p, i, lg: (2 * heads + h, nblk - 1 - i * p, 0)),
            pl.BlockSpec((1, dv), lambda h, p, i, lg: (0, h)),
        ],
        out_specs=pl.BlockSpec((blk, dv), lambda h, p, i, lg: (nblk - 1 - i * p, h)),
        scratch_shapes=[pltpu.VMEM((s, dv), F32), pltpu.VMEM((dk, dv), F32)],
    )
    return pl.pallas_call(
        functools.partial(_ret_kernel, chunk=chunk, nsub=blk // chunk),
        grid_spec=grid_spec,
        out_shape=jax.ShapeDtypeStruct((s, 2 * d), BF16),
        compiler_params=_params("arbitrary", "arbitrary", "arbitrary"),
        name="retention",
    )(lg, proj, proj, proj, proj, gn_g)


def _dft_tables(n):
    j = np.arange(n)
    ang = 2.0 * np.pi * ((j[:, None] * j[None, :]) % n) / n
    return np.cos(ang), np.sin(ang)


def _fourier_stage1_kernel(h_ref, cs_ref, ma_ref, mb_ref, wr_ref, wi_ref, o_ref, *, tile):
    n1, jb, dg = h_ref.shape
    r = jnp.dot(h_ref[...].reshape(n1 * jb, dg), cs_ref[...], preferred_element_type=F32)
    r = r.astype(BF16).reshape(n1, jb, 2 * dg)
    half = n1 * tile
    for s in range(jb // tile):
        rs = r[:, s * tile:(s + 1) * tile, :].reshape(half, 2 * dg)
        t = (jnp.dot(ma_ref[...], rs[:, :dg], preferred_element_type=F32)
             + jnp.dot(mb_ref[...], rs[:, dg:], preferred_element_type=F32))
        tr, ti = t[:half], t[half:]
        wr, wi = wr_ref[s], wi_ref[s]
        rows = pl.ds(s * tile, tile)
        o_ref[0, :, rows, :] = (tr * wr - ti * wi).astype(o_ref.dtype).reshape(n1, tile, dg)
        o_ref[1, :, rows, :] = (tr * wi + ti * wr).astype(o_ref.dtype).reshape(n1, tile, dg)


def _fourier_stage2_kernel(t_ref, c_ref, s_ref, o_ref):
    n1 = t_ref.shape[1]
    res = [jnp.dot(c_ref[...], t_ref[0, kk], preferred_element_type=F32)
           + jnp.dot(s_ref[...], t_ref[1, kk], preferred_element_type=F32) for kk in range(n1)]
    o_ref[...] = jnp.swapaxes(jnp.stack(res), 0, 1).astype(o_ref.dtype)


def _fourier_real(h, groups):
    s, d = h.shape
    dg = d // groups
    n1 = DFT_N1
    n2 = s // n1
    tile = BF16_SUBLANE_TILE
    jb = min(DFT_STEP_ROWS, n2)

    cd, sd = _dft_tables(dg)
    cs = jnp.asarray(np.concatenate([cd, sd], axis=1), F32).astype(BF16)
    c1, s1 = _dft_tables(n1)
    eye = np.eye(tile)
    ma = jnp.asarray(np.kron(np.concatenate([c1, -s1], axis=0), eye), F32).astype(BF16)
    mb = jnp.asarray(np.kron(np.concatenate([-s1, -c1], axis=0), eye), F32).astype(BF16)
    tw = 2.0 * np.pi * (np.arange(n1)[None, :, None]
                        * (np.arange(n2 // tile)[:, None, None] * tile + np.arange(tile)[None, None, :])) / s
    tw = tw.reshape(n2 // tile, n1 * tile, 1)
    wr = jnp.asarray(np.cos(tw), F32)
    wi = jnp.asarray(-np.sin(tw), F32)
    tp = pl.pallas_call(
        functools.partial(_fourier_stage1_kernel, tile=tile),
        grid=(n2 // jb, groups),
        in_specs=[pl.BlockSpec((n1, jb, dg), lambda j, g: (0, j, g)),
                  pl.BlockSpec((dg, 2 * dg), lambda j, g: (0, 0)),
                  pl.BlockSpec((2 * n1 * tile, n1 * tile), lambda j, g: (0, 0)),
                  pl.BlockSpec((2 * n1 * tile, n1 * tile), lambda j, g: (0, 0)),
                  pl.BlockSpec((jb // tile, n1 * tile, 1), lambda j, g: (j, 0, 0)),
                  pl.BlockSpec((jb // tile, n1 * tile, 1), lambda j, g: (j, 0, 0))],
        out_specs=pl.BlockSpec((2, n1, jb, dg), lambda j, g: (0, 0, j, g)),
        out_shape=jax.ShapeDtypeStruct((2, n1, n2, d), BF16),
        compiler_params=_params("parallel", "parallel"),
        name="fourier_stage1",
    )(h.reshape(n1, n2, d), cs, ma, mb, wr, wi)

    c2, s2 = _dft_tables(n2)
    db = min(DFT_STEP_COLS, d)
    mixed = pl.pallas_call(
        _fourier_stage2_kernel,
        grid=(d // db,),
        in_specs=[pl.BlockSpec((2, n1, n2, db), lambda e: (0, 0, 0, e)),
                  pl.BlockSpec((n2, n2), lambda e: (0, 0)),
                  pl.BlockSpec((n2, n2), lambda e: (0, 0))],
        out_specs=pl.BlockSpec((n2, n1, db), lambda e: (0, 0, e)),
        out_shape=jax.ShapeDtypeStruct((n2, n1, d), BF16),
        compiler_params=_params("parallel"),
        name="fourier_stage2",
    )(tp, jnp.asarray(c2, F32).astype(BF16), jnp.asarray(s2, F32).astype(BF16))
    return mixed.reshape(s, d)


def kernel(x, c, ada_w, ada_b, norm_g, ret_w_in, ret_w_out, ret_gn_g, ret_decay_fwd,
           ret_decay_bwd, fno_w, fno_b, mlp_w1, mlp_w2):
    batch, s, d = x.shape
    assert batch == 1
    depth = ada_w.shape[0]
    heads, groups = RET_HEADS, FNO_GROUPS
    x = x.reshape(s, d)

    assert depth == 2
    c_col = c.reshape(d, 1)
    mods = [_adaln(c_col, ada_w, ada_b, 2 * d)[0], None]

    def mod(layer, idx):
        return mods[layer][:, idx * d:(idx + 1) * d]

    def gain(layer, idx):
        return norm_g[layer, idx].reshape(1, d)

    half = d // heads // 2
    inv = ROPE_BASE ** (-jnp.arange(half, dtype=F32) / half)
    ang = jnp.arange(s, dtype=F32)[:, None] * inv[None, :]
    cos, sin = jnp.cos(ang), jnp.sin(ang)

    (h,) = _row_call(x, gx=gain(0, 0), sc=mod(0, 1), sh=mod(0, 0))
    for layer in range(depth):
        occ = layer // 2
        if layer % 2 == 0:
            dk = d // heads
            proj, rest = _matmul1(h, ret_w_in, occ, rope=(cos, sin, dk, float(dk) ** -0.5),
                                  slab=dk, side=(c_col, ada_w, ada_b, 2 * d))
            mods = [jnp.concatenate([mods[0], rest[:, :4 * d]], axis=1), rest[:, 4 * d:]]
            lg = jnp.stack([jax.nn.log_sigmoid(ret_decay_fwd[occ].astype(F32)),
                            jax.nn.log_sigmoid(ret_decay_bwd[occ].astype(F32))])
            yh = _retention(proj, lg, ret_gn_g[occ].reshape(1, 2 * d), heads)
            y = _matmulk(yh, ret_w_out, occ)
        else:
            mixed = _fourier_real(h, groups)
            y = _matmul1(mixed, fno_w, occ, bias=fno_b[occ].reshape(1, d))
        x, h = _row_call(x, y=y, gate=mod(layer, 2), gy=gain(layer, 1),
                         gx=gain(layer, 2), sc=mod(layer, 4), sh=mod(layer, 3))
        a = _matmul1(h, mlp_w1, layer, act="relu2")
        y = _matmulk(a, mlp_w2, layer)
        if layer + 1 < depth:
            x, h = _row_call(x, y=y, gate=mod(layer, 5), gy=gain(layer, 3),
                             gx=gain(layer + 1, 0), sc=mod(layer + 1, 1), sh=mod(layer + 1, 0))
        else:
            (x,) = _row_call(x, y=y, gate=mod(layer, 5), gy=gain(layer, 3))
    return x.reshape(batch, s, d)
```

```python
import functools

import numpy as np
import jax
import jax.numpy as jnp
from jax import lax
from jax.experimental import pallas as pl
from jax.experimental.pallas import tpu as pltpu

N_MOD = 6
RET_HEADS = 16
FNO_GROUPS = 8
ROPE_BASE = 10000.0
EPS = 1e-6

V7X_VMEM_BYTES = 64 * 1024 * 1024
V7X_LANES = 128
BF16_SUBLANE_TILE = 16
VMEM_LIMIT_BYTES = V7X_VMEM_BYTES - 4 * 1024 * 1024

RET_CHUNK = 256
RET_BLOCK = 4096
DFT_N1 = 16
DFT_STEP_ROWS = 128
DFT_STEP_COLS = 256
ROW_BLOCK = 512
ADALN_COLS = 512
MM1_BLOCK = (512, 2048)
MM1_BLOCK_EPILOGUE = (1024, 1024)
MMK_BLOCK = (2048, 1024, 2048)
MM1_ROW_SUB = 512
MMK_ROW_SUB = 512
SIDE_COLS = 256

F32 = jnp.float32
BF16 = jnp.bfloat16


def _params(*sem):
    return pltpu.CompilerParams(dimension_semantics=sem, vmem_limit_bytes=VMEM_LIMIT_BYTES)


def _adaln_kernel(c_ref, w_ref, b_ref, o_ref):
    c = c_ref[...]
    s = c * jax.nn.sigmoid(c)
    o_ref[0] = jnp.sum(w_ref[0] * s, axis=0, keepdims=True) + b_ref[0]


def _adaln(c_col, ada_w, ada_b, ncols):
    depth, d, n = ada_w.shape
    bn = min(ADALN_COLS, ncols)
    return pl.pallas_call(
        _adaln_kernel,
        grid=(1, ncols // bn),
        in_specs=[pl.BlockSpec((d, 1), lambda l, j: (0, 0)),
                  pl.BlockSpec((1, d, bn), lambda l, j: (l, 0, j)),
                  pl.BlockSpec((1, 1, bn), lambda l, j: (l, 0, j))],
        out_specs=pl.BlockSpec((1, 1, bn), lambda l, j: (l, 0, j)),
        out_shape=jax.ShapeDtypeStruct((1, 1, ncols), F32),
        compiler_params=_params("parallel", "parallel"),
        name="adaln",
    )(c_col, ada_w, ada_b.reshape(depth, 1, n))


def _rms(x, g):
    ms = jnp.mean(x * x, axis=-1, keepdims=True)
    return x * lax.rsqrt(ms + EPS) * g


def _row_kernel(*refs, has_y, has_h):
    refs = list(refs)
    x_ref = refs.pop(0)
    x = x_ref[...]
    if has_y:
        y_ref, gate_ref, gy_ref = refs.pop(0), refs.pop(0), refs.pop(0)
    if has_h:
        gx_ref, sc_ref, sh_ref = refs.pop(0), refs.pop(0), refs.pop(0)
    if has_y:
        xo_ref = refs.pop(0)
        x = x + gate_ref[...] * _rms(y_ref[...].astype(F32), gy_ref[...])
        xo_ref[...] = x
    if has_h:
        h_ref = refs.pop(0)
        h_ref[...] = (_rms(x, gx_ref[...]) * (1.0 + sc_ref[...]) + sh_ref[...]).astype(BF16)


def _row_call(x, y=None, gate=None, gy=None, gx=None, sc=None, sh=None):
    s, d = x.shape
    bm = min(ROW_BLOCK, s)
    has_y, has_h = y is not None, gx is not None
    row = pl.BlockSpec((bm, d), lambda i: (i, 0))
    vec = pl.BlockSpec((1, d), lambda i: (0, 0))
    args, in_specs, out_specs, out_shape = [x], [row], [], []
    if has_y:
        args += [y, gate, gy]
        in_specs += [row, vec, vec]
        out_specs.append(row)
        out_shape.append(jax.ShapeDtypeStruct((s, d), F32))
    if has_h:
        args += [gx, sc, sh]
        in_specs += [vec, vec, vec]
        out_specs.append(row)
        out_shape.append(jax.ShapeDtypeStruct((s, d), BF16))
    out = pl.pallas_call(
        functools.partial(_row_kernel, has_y=has_y, has_h=has_h),
        grid=(s // bm,),
        in_specs=in_specs, out_specs=out_specs, out_shape=out_shape,
        compiler_params=_params("parallel"),
        name="row_norm",
    )(*args)
    return out


def _mmk_kernel(a_ref, w_ref, o_ref, wb_ref, acc_ref, *, nk, ck, bm):
    b, m = pl.program_id(0), pl.program_id(1)
    k = (b - 1) % nk
    rb = min(MMK_ROW_SUB, bm)

    def cast_next():
        wb_ref[b % 2, pl.ds(pl.multiple_of(m * ck, ck), ck), :] = w_ref[...].astype(BF16)

    def body(first, last):
        cast_next()
        wb = wb_ref.at[(b + 1) % 2]
        for r in range(bm // rb):
            part = jnp.dot(a_ref[r * rb:(r + 1) * rb, :], wb[...], preferred_element_type=F32)
            rows = pl.ds(pl.multiple_of(m * bm + r * rb, rb), rb)
            if not first:
                part = acc_ref[rows, :] + part
            if last:
                o_ref[r * rb:(r + 1) * rb, :] = part.astype(o_ref.dtype)
            else:
                acc_ref[rows, :] = part

    @pl.when(b == 0)
    def _():
        cast_next()

    @pl.when(jnp.logical_and(b > 0, k == 0))
    def _():
        body(True, False)

    @pl.when(jnp.logical_and(b > 0, jnp.logical_and(k > 0, k < nk - 1)))
    def _():
        body(False, False)

    @pl.when(jnp.logical_and(b > 0, k == nk - 1))
    def _():
        body(False, True)


def _matmulk(a, w, layer, halves=2):
    m, kd = a.shape
    _, _, n = w.shape
    bm, bn, bk = MMK_BLOCK
    nb, nk = n // bn, kd // bk
    mb = m // halves // bm
    ck = bk // mb
    nblocks = halves * nb * nk
    assert nk > 1 and mb * bm * halves == m and ck * mb == bk

    def dec(x):
        return x // (nk * nb), (x // nk) % nb, x % nk

    def a_map(b, j):
        hh, _, kk = dec(jnp.maximum(b - 1, 0))
        return hh * mb + jnp.where(b > 0, j, 0), kk

    def w_map(b, j):
        _, nn, kk = dec(jnp.minimum(b, nblocks - 1))
        return layer, kk * mb + j, nn

    def o_map(b, j):
        hh, nn, kk = dec(jnp.maximum(b - 1, 0))
        return hh * mb + jnp.where(jnp.logical_and(b > 0, kk == nk - 1), j, 0), nn

    return pl.pallas_call(
        functools.partial(_mmk_kernel, nk=nk, ck=ck, bm=bm),
        grid=(nblocks + 1, mb),
        in_specs=[pl.BlockSpec((bm, bk), a_map),
                  pl.BlockSpec((None, ck, bn), w_map)],
        out_specs=pl.BlockSpec((bm, bn), o_map),
        out_shape=jax.ShapeDtypeStruct((m, n), BF16),
        scratch_shapes=[pltpu.VMEM((2, bk, bn), BF16), pltpu.VMEM((m // halves, bn), F32)],
        compiler_params=_params("arbitrary", "arbitrary"),
        name="matmulk",
    )(a, w)


def _mm1_kernel(*refs, nb, ck, act, has_bias, rope, side):
    refs = list(refs)
    a_ref, w_ref = refs.pop(0), refs.pop(0)
    b_ref = refs.pop(0) if has_bias else None
    cos_ref, sin_ref = (refs.pop(0), refs.pop(0)) if rope else (None, None)
    if side:
        c_ref, aw_ref, ab_ref = refs.pop(0), refs.pop(0), refs.pop(0)
        o_ref, mo_ref, wb_ref, sb_ref = refs
    else:
        o_ref, wb_ref = refs
    n, m = pl.program_id(0), pl.program_id(1)

    def cast_next():
        wb_ref[n % 2, pl.ds(pl.multiple_of(m * ck, ck), ck), :] = w_ref[...].astype(BF16)

    def epilogue(acc):
        if has_bias:
            acc = acc + b_ref[...]
        if act == "relu2":
            acc = jnp.maximum(acc, 0.0)
            acc = acc * acc
        return acc

    def compute(rotary):
        cast_next()
        if side:
            lanes = sb_ref.shape[1]
            mv = [jnp.sum(aw_ref[:, t * lanes:(t + 1) * lanes] * sb_ref[...], axis=0, keepdims=True)
                  for t in range(aw_ref.shape[1] // lanes)]
            mo_ref[...] = jnp.concatenate(mv, axis=-1) + ab_ref[...]
        bm = a_ref.shape[0]
        rb = min(MM1_ROW_SUB, bm)
        wb = wb_ref.at[(n + 1) % 2]
        if rotary:
            nq, dk, k_scale = rope
            scale = jnp.where(n - 1 >= nq, k_scale, 1.0)
        for r in range(bm // rb):
            rows = slice(r * rb, (r + 1) * rb)
            acc = epilogue(jnp.dot(a_ref[rows, :], wb[...], preferred_element_type=F32))
            if rotary:
                c, s = cos_ref[rows, :] * scale, sin_ref[rows, :] * scale
                half = dk // 2
                parts = []
                for hh in range(acc.shape[1] // dk):
                    x1 = acc[:, hh * dk:hh * dk + half]
                    x2 = acc[:, hh * dk + half:(hh + 1) * dk]
                    parts += [x1 * c - x2 * s, x1 * s + x2 * c]
                acc = jnp.concatenate(parts, axis=-1)
            if len(o_ref.shape) == 3:
                slab = o_ref.shape[2]
                for hh in range(o_ref.shape[0]):
                    o_ref[hh, rows, :] = acc[:, hh * slab:(hh + 1) * slab].astype(o_ref.dtype)
            else:
                o_ref[rows, :] = acc.astype(o_ref.dtype)

    @pl.when(n == 0)
    def _():
        cast_next()

    if side:
        @pl.when(jnp.logical_and(n == 0, m == 0))
        def _():
            c = c_ref[...]
            sb_ref[...] = jnp.broadcast_to(c * jax.nn.sigmoid(c), sb_ref.shape)

    if rope:
        nqk = 2 * rope[0]

        @pl.when(jnp.logical_and(n > 0, n - 1 < nqk))
        def _():
            compute(True)

        @pl.when(n - 1 >= nqk)
        def _():
            compute(False)
    else:
        @pl.when(n > 0)
        def _():
            compute(False)


def _matmul1(a, w, layer, bias=None, act=None, out_dtype=BF16, rope=None, slab=None, side=None,
             block=MM1_BLOCK):
    m, kd = a.shape
    _, _, n = w.shape
    bm, bn = min(block[0], m), min(block[1], n)
    nb, mb = n // bn, m // bm
    ck = kd // mb
    has_bias = bias is not None

    def row(i, j):
        return jnp.where(i > 0, j, 0)

    def col(i):
        return jnp.maximum(i - 1, 0)

    in_specs = [pl.BlockSpec((bm, kd), lambda i, j: (row(i, j), 0)),
                pl.BlockSpec((None, ck, bn), lambda i, j: (layer, j, jnp.minimum(i, nb - 1)))]
    args = [a, w]
    if has_bias:
        in_specs.append(pl.BlockSpec((1, bn), lambda i, j: (0, col(i))))
        args.append(bias)
    rope_static = None
    if rope is not None:
        cos, sin, dk, k_scale = rope
        in_specs += [pl.BlockSpec((bm, dk // 2), lambda i, j: (row(i, j), 0))] * 2
        args += [cos, sin]
        rope_static = (dk * (n // 6 // dk) // bn, dk, k_scale)
    if slab is None:
        out_specs = [pl.BlockSpec((bm, bn), lambda i, j: (row(i, j), col(i)))]
        out_shape = [jax.ShapeDtypeStruct((m, n), out_dtype)]
    else:
        out_specs = [pl.BlockSpec((bn // slab, bm, slab), lambda i, j: (col(i), row(i, j), 0))]
        out_shape = [jax.ShapeDtypeStruct((n // slab, m, slab), out_dtype)]
    scratch = [pltpu.VMEM((2, kd, bn), BF16)]
    if side is not None:
        c_col, ada_w, ada_b, first_col = side
        layers, da, na = ada_w.shape
        todo = layers * na - first_col
        sw = SIDE_COLS
        nside = todo // sw
        assert nside * sw == todo and first_col % sw == 0 and na % sw == 0 and nside <= nb * mb

        def sblk(i, j):
            return jnp.minimum(col(i) * mb + row(i, j), nside - 1)

        def flat(i, j):
            return first_col // sw + sblk(i, j)

        per_layer = na // sw
        in_specs += [pl.BlockSpec((da, 1), lambda i, j: (0, 0)),
                     pl.BlockSpec((None, da, sw), lambda i, j: (flat(i, j) // per_layer, 0, flat(i, j) % per_layer)),
                     pl.BlockSpec((None, 1, sw), lambda i, j: (flat(i, j) // per_layer, 0, flat(i, j) % per_layer))]
        args += [c_col, ada_w, ada_b.reshape(layers, 1, na)]
        out_specs.append(pl.BlockSpec((1, sw), lambda i, j: (0, sblk(i, j))))
        out_shape.append(jax.ShapeDtypeStruct((1, todo), F32))
        scratch.append(pltpu.VMEM((da, V7X_LANES), F32))
    out = pl.pallas_call(
        functools.partial(_mm1_kernel, nb=nb, ck=ck, act=act, has_bias=has_bias, rope=rope_static,
                          side=side is not None),
        grid=(nb + 1, mb),
        in_specs=in_specs,
        out_specs=out_specs,
        out_shape=out_shape,
        scratch_shapes=scratch,
        compiler_params=_params("arbitrary", "arbitrary"),
        name="matmul1",
    )(*args)
    return out if side is not None else out[0]


def _ret_kernel(lg_ref, q_ref, k_ref, v_ref, g_ref, gn_ref,
                o_ref, yacc_ref, st_ref, *, chunk, nsub):
    h, p, i = pl.program_id(0), pl.program_id(1), pl.program_id(2)
    nblk = pl.num_programs(2)
    blk_rows = chunk * nsub
    dk = q_ref.shape[1]
    lgf, lgb = lg_ref[0, h], lg_ref[1, h]

    @pl.when(i == 0)
    def _():
        st_ref[...] = jnp.zeros_like(st_ref)

    r = lax.broadcasted_iota(jnp.int32, (chunk, 1), 0).astype(F32)

    def wide(ref, rows):
        return jnp.concatenate([ref[0, rows, :], ref[1, rows, :]], axis=-1)

    def decay(expo):
        return jnp.broadcast_to(jnp.exp(expo), (chunk, dk)).astype(BF16)

    def state_update(k, kdec, v, lg):
        upd = lax.dot_general(k * kdec, v, (((0,), (0,)), ((), ())), preferred_element_type=F32)
        st_ref[...] = st_ref[...] * jnp.exp(jnp.full((1, 1), chunk * lg, F32)) + upd

    @pl.when(p == 0)
    def _():
        ri = lax.broadcasted_iota(jnp.int32, (chunk, chunk), 0)
        ci = lax.broadcasted_iota(jnp.int32, (chunk, chunk), 1)
        diff = (ri - ci).astype(F32)
        dmat = (jnp.where(diff >= 0, jnp.exp(jnp.maximum(diff, 0.0) * lgf), 0.0)
                + jnp.where(diff <= 0, jnp.exp(jnp.maximum(-diff, 0.0) * lgb), 0.0))
        qdec = decay((r + 1.0) * lgf)
        kdec = decay((chunk - 1.0 - r) * lgf)
        for s in range(nsub):
            rows = pl.ds(s * chunk, chunk)
            q, k, v = q_ref[rows, :], k_ref[rows, :], wide(v_ref, rows)
            sc = lax.dot_general(q, k, (((1,), (1,)), ((), ())), preferred_element_type=F32)
            intra = jnp.dot((sc * dmat).astype(BF16), v, preferred_element_type=F32)
            cross = jnp.dot(q * qdec, st_ref[...].astype(BF16), preferred_element_type=F32)
            yacc_ref[pl.ds(pl.multiple_of(i * blk_rows + s * chunk, chunk), chunk), :] = intra + cross
            state_update(k, kdec, v, lgf)

    @pl.when(p == 1)
    def _():
        qdec = decay((chunk - r) * lgb)
        kdec = decay(r * lgb)
        base = (nblk - 1 - i) * blk_rows
        for s in reversed(range(nsub)):
            rows = pl.ds(s * chunk, chunk)
            q, k, v = q_ref[rows, :], k_ref[rows, :], wide(v_ref, rows)
            cross = jnp.dot(q * qdec, st_ref[...].astype(BF16), preferred_element_type=F32)
            y = yacc_ref[pl.ds(pl.multiple_of(base + s * chunk, chunk), chunk), :] + cross
            mu = jnp.mean(y, axis=-1, keepdims=True)
            yc = y - mu
            var = jnp.mean(yc * yc, axis=-1, keepdims=True)
            yn = yc * lax.rsqrt(var + EPS) * gn_ref[...]
            g = wide(g_ref, rows).astype(F32)
            o_ref[rows, :] = (g * jax.nn.sigmoid(g) * yn).astype(o_ref.dtype)
            state_update(k, kdec, v, lgb)


def _retention(proj, lg, gn_g, heads):
    nslab, s, dk = proj.shape
    d = nslab * dk // 6
    dv = 2 * dk
    chunk = min(RET_CHUNK, s)
    blk = min(RET_BLOCK, s)
    nblk = s // blk

    def cidx(p, i):
        return i + p * (nblk - 1 - 2 * i)

    grid_spec = pltpu.PrefetchScalarGridSpec(
        num_scalar_prefetch=1,
        grid=(heads, 2, nblk),
        in_specs=[
            pl.BlockSpec((None, blk, dk), lambda h, p, i, lg: (h, cidx(p, i), 0)),
            pl.BlockSpec((None, blk, dk), lambda h, p, i, lg: (heads + h, cidx(p, i), 0)),
            pl.BlockSpec((2, blk, dk), lambda h, p, i, lg: (heads + h, cidx(p, i), 0)),
            pl.BlockSpec((2, blk, dk), lambda h, p, i, lg: (2 * heads + h, nblk - 1 - i * p, 0)),
            pl.BlockSpec((1, dv), lambda h, p, i, lg: (0, h)),
        ],
        out_specs=pl.BlockSpec((blk, dv), lambda h, p, i, lg: (nblk - 1 - i * p, h)),
        scratch_shapes=[pltpu.VMEM((s, dv), F32), pltpu.VMEM((dk, dv), F32)],
    )
    return pl.pallas_call(
        functools.partial(_ret_kernel, chunk=chunk, nsub=blk // chunk),
        grid_spec=grid_spec,
        out_shape=jax.ShapeDtypeStruct((s, 2 * d), BF16),
        compiler_params=_params("arbitrary", "arbitrary", "arbitrary"),
        name="retention",
    )(lg, proj, proj, proj, proj, gn_g)


def _dft_tables(n):
    j = np.arange(n)
    ang = 2.0 * np.pi * ((j[:, None] * j[None, :]) % n) / n
    return np.cos(ang), np.sin(ang)


def _fourier_stage1_kernel(h_ref, cs_ref, ma_ref, mb_ref, wr_ref, wi_ref, o_ref, *, tile):
    n1, jb, dg = h_ref.shape
    r = jnp.dot(h_ref[...].reshape(n1 * jb, dg), cs_ref[...], preferred_element_type=F32)
    r = r.astype(BF16).reshape(n1, jb, 2 * dg)
    half = n1 * tile
    for s in range(jb // tile):
        rs = r[:, s * tile:(s + 1) * tile, :].reshape(half, 2 * dg)
        t = (jnp.dot(ma_ref[...], rs[:, :dg], preferred_element_type=F32)
             + jnp.dot(mb_ref[...], rs[:, dg:], preferred_element_type=F32))
        tr, ti = t[:half], t[half:]
        wr, wi = wr_ref[s], wi_ref[s]
        rows = pl.ds(s * tile, tile)
        o_ref[0, :, rows, :] = (tr * wr - ti * wi).astype(o_ref.dtype).reshape(n1, tile, dg)
        o_ref[1, :, rows, :] = (tr * wi + ti * wr).astype(o_ref.dtype).reshape(n1, tile, dg)


def _fourier_stage2_kernel(t_ref, c_ref, s_ref, o_ref):
    n1 = t_ref.shape[1]
    res = [jnp.dot(c_ref[...], t_ref[0, kk], preferred_element_type=F32)
           + jnp.dot(s_ref[...], t_ref[1, kk], preferred_element_type=F32) for kk in range(n1)]
    o_ref[...] = jnp.swapaxes(jnp.stack(res), 0, 1).astype(o_ref.dtype)


def _fourier_real(h, groups):
    s, d = h.shape
    dg = d // groups
    n1 = DFT_N1
    n2 = s // n1
    tile = BF16_SUBLANE_TILE
    jb = min(DFT_STEP_ROWS, n2)

    cd, sd = _dft_tables(dg)
    cs = jnp.asarray(np.concatenate([cd, sd], axis=1), F32).astype(BF16)
    c1, s1 = _dft_tables(n1)
    eye = np.eye(tile)
    ma = jnp.asarray(np.kron(np.concatenate([c1, -s1], axis=0), eye), F32).astype(BF16)
    mb = jnp.asarray(np.kron(np.concatenate([-s1, -c1], axis=0), eye), F32).astype(BF16)
    tw = 2.0 * np.pi * (np.arange(n1)[None, :, None]
                        * (np.arange(n2 // tile)[:, None, None] * tile + np.arange(tile)[None, None, :])) / s
    tw = tw.reshape(n2 // tile, n1 * tile, 1)
    wr = jnp.asarray(np.cos(tw), F32)
    wi = jnp.asarray(-np.sin(tw), F32)
    tp = pl.pallas_call(
        functools.partial(_fourier_stage1_kernel, tile=tile),
        grid=(n2 // jb, groups),
        in_specs=[pl.BlockSpec((n1, jb, dg), lambda j, g: (0, j, g)),
                  pl.BlockSpec((dg, 2 * dg), lambda j, g: (0, 0)),
                  pl.BlockSpec((2 * n1 * tile, n1 * tile), lambda j, g: (0, 0)),
                  pl.BlockSpec((2 * n1 * tile, n1 * tile), lambda j, g: (0, 0)),
                  pl.BlockSpec((jb // tile, n1 * tile, 1), lambda j, g: (j, 0, 0)),
                  pl.BlockSpec((jb // tile, n1 * tile, 1), lambda j, g: (j, 0, 0))],
        out_specs=pl.BlockSpec((2, n1, jb, dg), lambda j, g: (0, 0, j, g)),
        out_shape=jax.ShapeDtypeStruct((2, n1, n2, d), BF16),
        compiler_params=_params("parallel", "parallel"),
        name="fourier_stage1",
    )(h.reshape(n1, n2, d), cs, ma, mb, wr, wi)

    c2, s2 = _dft_tables(n2)
    db = min(DFT_STEP_COLS, d)
    mixed = pl.pallas_call(
        _fourier_stage2_kernel,
        grid=(d // db,),
        in_specs=[pl.BlockSpec((2, n1, n2, db), lambda e: (0, 0, 0, e)),
                  pl.BlockSpec((n2, n2), lambda e: (0, 0)),
                  pl.BlockSpec((n2, n2), lambda e: (0, 0))],
        out_specs=pl.BlockSpec((n2, n1, db), lambda e: (0, 0, e)),
        out_shape=jax.ShapeDtypeStruct((n2, n1, d), BF16),
        compiler_params=_params("parallel"),
        name="fourier_stage2",
    )(tp, jnp.asarray(c2, F32).astype(BF16), jnp.asarray(s2, F32).astype(BF16))
    return mixed.reshape(s, d)


def kernel(x, c, ada_w, ada_b, norm_g, ret_w_in, ret_w_out, ret_gn_g, ret_decay_fwd,
           ret_decay_bwd, fno_w, fno_b, mlp_w1, mlp_w2):
    batch, s, d = x.shape
    assert batch == 1
    depth = ada_w.shape[0]
    heads, groups = RET_HEADS, FNO_GROUPS
    x = x.reshape(s, d)

    assert depth == 2
    c_col = c.reshape(d, 1)
    mods = [_adaln(c_col, ada_w, ada_b, 2 * d)[0], None]

    def mod(layer, idx):
        return mods[layer][:, idx * d:(idx + 1) * d]

    def gain(layer, idx):
        return norm_g[layer, idx].reshape(1, d)

    half = d // heads // 2
    inv = ROPE_BASE ** (-jnp.arange(half, dtype=F32) / half)
    ang = jnp.arange(s, dtype=F32)[:, None] * inv[None, :]
    cos, sin = jnp.cos(ang), jnp.sin(ang)

    (h,) = _row_call(x, gx=gain(0, 0), sc=mod(0, 1), sh=mod(0, 0))
    for layer in range(depth):
        occ = layer // 2
        if layer % 2 == 0:
            dk = d // heads
            proj, rest = _matmul1(h, ret_w_in, occ, rope=(cos, sin, dk, float(dk) ** -0.5),
                                  slab=dk, side=(c_col, ada_w, ada_b, 2 * d), block=MM1_BLOCK_EPILOGUE)
            mods = [jnp.concatenate([mods[0], rest[:, :4 * d]], axis=1), rest[:, 4 * d:]]
            lg = jnp.stack([jax.nn.log_sigmoid(ret_decay_fwd[occ].astype(F32)),
                            jax.nn.log_sigmoid(ret_decay_bwd[occ].astype(F32))])
            yh = _retention(proj, lg, ret_gn_g[occ].reshape(1, 2 * d), heads)
            y = _matmulk(yh, ret_w_out, occ)
        else:
            mixed = _fourier_real(h, groups)
            y = _matmul1(mixed, fno_w, occ, bias=fno_b[occ].reshape(1, d))
        x, h = _row_call(x, y=y, gate=mod(layer, 2), gy=gain(layer, 1),
                         gx=gain(layer, 2), sc=mod(layer, 4), sh=mod(layer, 3))
        a = _matmul1(h, mlp_w1, layer, act="relu2")
        y = _matmulk(a, mlp_w2, layer)
        if layer + 1 < depth:
            x, h = _row_call(x, y=y, gate=mod(layer, 5), gy=gain(layer, 3),
                             gx=gain(layer + 1, 0), sc=mod(layer + 1, 1), sh=mod(layer + 1, 0))
        else:
            (x,) = _row_call(x, y=y, gate=mod(layer, 5), gy=gain(layer, 3))
    return x.reshape(batch, s, d)
```

```python
import functools

import numpy as np
import jax
import jax.numpy as jnp
from jax import lax
from jax.experimental import pallas as pl
from jax.experimental.pallas import tpu as pltpu

N_MOD = 6
RET_HEADS = 16
FNO_GROUPS = 8
ROPE_BASE = 10000.0
EPS = 1e-6

V7X_VMEM_BYTES = 64 * 1024 * 1024
V7X_LANES = 128
BF16_SUBLANE_TILE = 16
VMEM_LIMIT_BYTES = V7X_VMEM_BYTES - 4 * 1024 * 1024

RET_CHUNK = 256
RET_BLOCK = 4096
DFT_N1 = 16
DFT_STEP_ROWS = 128
DFT_STEP_COLS = 256
ROW_BLOCK = 512
ADALN_COLS = 512
MM1_BLOCK = (1024, 1024)
MMK_BLOCK = (2048, 1024, 2048)
MM1_ROW_SUB = 512
MMK_ROW_SUB = 1024
SIDE_COLS = 256

F32 = jnp.float32
BF16 = jnp.bfloat16


def _params(*sem):
    return pltpu.CompilerParams(dimension_semantics=sem, vmem_limit_bytes=VMEM_LIMIT_BYTES)


def _adaln_kernel(c_ref, w_ref, b_ref, o_ref):
    c = c_ref[...]
    s = c * jax.nn.sigmoid(c)
    o_ref[0] = jnp.sum(w_ref[0] * s, axis=0, keepdims=True) + b_ref[0]


def _adaln(c_col, ada_w, ada_b, ncols):
    depth, d, n = ada_w.shape
    bn = min(ADALN_COLS, ncols)
    return pl.pallas_call(
        _adaln_kernel,
        grid=(1, ncols // bn),
        in_specs=[pl.BlockSpec((d, 1), lambda l, j: (0, 0)),
                  pl.BlockSpec((1, d, bn), lambda l, j: (l, 0, j)),
                  pl.BlockSpec((1, 1, bn), lambda l, j: (l, 0, j))],
        out_specs=pl.BlockSpec((1, 1, bn), lambda l, j: (l, 0, j)),
        out_shape=jax.ShapeDtypeStruct((1, 1, ncols), F32),
        compiler_params=_params("parallel", "parallel"),
        name="adaln",
    )(c_col, ada_w, ada_b.reshape(depth, 1, n))


def _rms(x, g):
    ms = jnp.mean(x * x, axis=-1, keepdims=True)
    return x * lax.rsqrt(ms + EPS) * g


def _row_kernel(*refs, has_y, has_h):
    refs = list(refs)
    x_ref = refs.pop(0)
    x = x_ref[...]
    if has_y:
        y_ref, gate_ref, gy_ref = refs.pop(0), refs.pop(0), refs.pop(0)
    if has_h:
        gx_ref, sc_ref, sh_ref = refs.pop(0), refs.pop(0), refs.pop(0)
    if has_y:
        xo_ref = refs.pop(0)
        x = x + gate_ref[...] * _rms(y_ref[...].astype(F32), gy_ref[...])
        xo_ref[...] = x
    if has_h:
        h_ref = refs.pop(0)
        h_ref[...] = (_rms(x, gx_ref[...]) * (1.0 + sc_ref[...]) + sh_ref[...]).astype(BF16)


def _row_call(x, y=None, gate=None, gy=None, gx=None, sc=None, sh=None):
    s, d = x.shape
    bm = min(ROW_BLOCK, s)
    has_y, has_h = y is not None, gx is not None
    row = pl.BlockSpec((bm, d), lambda i: (i, 0))
    vec = pl.BlockSpec((1, d), lambda i: (0, 0))
    args, in_specs, out_specs, out_shape = [x], [row], [], []
    if has_y:
        args += [y, gate, gy]
        in_specs += [row, vec, vec]
        out_specs.append(row)
        out_shape.append(jax.ShapeDtypeStruct((s, d), F32))
    if has_h:
        args += [gx, sc, sh]
        in_specs += [vec, vec, vec]
        out_specs.append(row)
        out_shape.append(jax.ShapeDtypeStruct((s, d), BF16))
    out = pl.pallas_call(
        functools.partial(_row_kernel, has_y=has_y, has_h=has_h),
        grid=(s // bm,),
        in_specs=in_specs, out_specs=out_specs, out_shape=out_shape,
        compiler_params=_params("parallel"),
        name="row_norm",
    )(*args)
    return out


def _mmk_kernel(a_ref, w_ref, o_ref, wb_ref, acc_ref, *, nk, ck, bm):
    b, m = pl.program_id(0), pl.program_id(1)
    k = (b - 1) % nk
    rb = min(MMK_ROW_SUB, bm)

    def cast_next():
        wb_ref[b % 2, pl.ds(pl.multiple_of(m * ck, ck), ck), :] = w_ref[...].astype(BF16)

    def body(first, last):
        cast_next()
        wb = wb_ref.at[(b + 1) % 2]
        for r in range(bm // rb):
            part = jnp.dot(a_ref[r * rb:(r + 1) * rb, :], wb[...], preferred_element_type=F32)
            rows = pl.ds(pl.multiple_of(m * bm + r * rb, rb), rb)
            if not first:
                part = acc_ref[rows, :] + part
            if last:
                o_ref[r * rb:(r + 1) * rb, :] = part.astype(o_ref.dtype)
            else:
                acc_ref[rows, :] = part

    @pl.when(b == 0)
    def _():
        cast_next()

    @pl.when(jnp.logical_and(b > 0, k == 0))
    def _():
        body(True, False)

    @pl.when(jnp.logical_and(b > 0, jnp.logical_and(k > 0, k < nk - 1)))
    def _():
        body(False, False)

    @pl.when(jnp.logical_and(b > 0, k == nk - 1))
    def _():
        body(False, True)


def _matmulk(a, w, layer, halves=2):
    m, kd = a.shape
    _, _, n = w.shape
    bm, bn, bk = MMK_BLOCK
    nb, nk = n // bn, kd // bk
    mb = m // halves // bm
    ck = bk // mb
    nblocks = halves * nb * nk
    assert nk > 1 and mb * bm * halves == m and ck * mb == bk

    def dec(x):
        return x // (nk * nb), (x // nk) % nb, x % nk

    def a_map(b, j):
        hh, _, kk = dec(jnp.maximum(b - 1, 0))
        return hh * mb + jnp.where(b > 0, j, 0), kk

    def w_map(b, j):
        _, nn, kk = dec(jnp.minimum(b, nblocks - 1))
        return layer, kk * mb + j, nn

    def o_map(b, j):
        hh, nn, kk = dec(jnp.maximum(b - 1, 0))
        return hh * mb + jnp.where(jnp.logical_and(b > 0, kk == nk - 1), j, 0), nn

    return pl.pallas_call(
        functools.partial(_mmk_kernel, nk=nk, ck=ck, bm=bm),
        grid=(nblocks + 1, mb),
        in_specs=[pl.BlockSpec((bm, bk), a_map),
                  pl.BlockSpec((None, ck, bn), w_map)],
        out_specs=pl.BlockSpec((bm, bn), o_map),
        out_shape=jax.ShapeDtypeStruct((m, n), BF16),
        scratch_shapes=[pltpu.VMEM((2, bk, bn), BF16), pltpu.VMEM((m // halves, bn), F32)],
        compiler_params=_params("arbitrary", "arbitrary"),
        name="matmulk",
    )(a, w)


def _mm1_kernel(*refs, nb, ck, act, has_bias, rope, side):
    refs = list(refs)
    a_ref, w_ref = refs.pop(0), refs.pop(0)
    b_ref = refs.pop(0) if has_bias else None
    cos_ref, sin_ref = (refs.pop(0), refs.pop(0)) if rope else (None, None)
    if side:
        c_ref, aw_ref, ab_ref = refs.pop(0), refs.pop(0), refs.pop(0)
        o_ref, mo_ref, wb_ref, sb_ref = refs
    else:
        o_ref, wb_ref = refs
    n, m = pl.program_id(0), pl.program_id(1)

    def cast_next():
        wb_ref[n % 2, pl.ds(pl.multiple_of(m * ck, ck), ck), :] = w_ref[...].astype(BF16)

    def epilogue(acc):
        if has_bias:
            acc = acc + b_ref[...]
        if act == "relu2":
            acc = jnp.maximum(acc, 0.0)
            acc = acc * acc
        return acc

    def compute(rotary):
        cast_next()
        if side:
            lanes = sb_ref.shape[1]
            mv = [jnp.sum(aw_ref[:, t * lanes:(t + 1) * lanes] * sb_ref[...], axis=0, keepdims=True)
                  for t in range(aw_ref.shape[1] // lanes)]
            mo_ref[...] = jnp.concatenate(mv, axis=-1) + ab_ref[...]
        bm = a_ref.shape[0]
        rb = min(MM1_ROW_SUB, bm)
        wb = wb_ref.at[(n + 1) % 2]
        if rotary:
            nq, dk, k_scale = rope
            scale = jnp.where(n - 1 >= nq, k_scale, 1.0)
        for r in range(bm // rb):
            rows = slice(r * rb, (r + 1) * rb)
            acc = epilogue(jnp.dot(a_ref[rows, :], wb[...], preferred_element_type=F32))
            if rotary:
                c, s = cos_ref[rows, :] * scale, sin_ref[rows, :] * scale
                half = dk // 2
                parts = []
                for hh in range(acc.shape[1] // dk):
                    x1 = acc[:, hh * dk:hh * dk + half]
                    x2 = acc[:, hh * dk + half:(hh + 1) * dk]
                    parts += [x1 * c - x2 * s, x1 * s + x2 * c]
                acc = jnp.concatenate(parts, axis=-1)
            if len(o_ref.shape) == 3:
                slab = o_ref.shape[2]
                for hh in range(o_ref.shape[0]):
                    o_ref[hh, rows, :] = acc[:, hh * slab:(hh + 1) * slab].astype(o_ref.dtype)
            else:
                o_ref[rows, :] = acc.astype(o_ref.dtype)

    @pl.when(n == 0)
    def _():
        cast_next()

    if side:
        @pl.when(jnp.logical_and(n == 0, m == 0))
        def _():
            c = c_ref[...]
            sb_ref[...] = jnp.broadcast_to(c * jax.nn.sigmoid(c), sb_ref.shape)

    if rope:
        nqk = 2 * rope[0]

        @pl.when(jnp.logical_and(n > 0, n - 1 < nqk))
        def _():
            compute(True)

        @pl.when(n - 1 >= nqk)
        def _():
            compute(False)
    else:
        @pl.when(n > 0)
        def _():
            compute(False)


def _matmul1(a, w, layer, bias=None, act=None, out_dtype=BF16, rope=None, slab=None, side=None):
    m, kd = a.shape
    _, _, n = w.shape
    bm, bn = min(MM1_BLOCK[0], m), min(MM1_BLOCK[1], n)
    nb, mb = n // bn, m // bm
    ck = kd // mb
    has_bias = bias is not None

    def row(i, j):
        return jnp.where(i > 0, j, 0)

    def col(i):
        return jnp.maximum(i - 1, 0)

    in_specs = [pl.BlockSpec((bm, kd), lambda i, j: (row(i, j), 0)),
                pl.BlockSpec((None, ck, bn), lambda i, j: (layer, j, jnp.minimum(i, nb - 1)))]
    args = [a, w]
    if has_bias:
        in_specs.append(pl.BlockSpec((1, bn), lambda i, j: (0, col(i))))
        args.append(bias)
    rope_static = None
    if rope is not None:
        cos, sin, dk, k_scale = rope
        in_specs += [pl.BlockSpec((bm, dk // 2), lambda i, j: (row(i, j), 0))] * 2
        args += [cos, sin]
        rope_static = (dk * (n // 6 // dk) // bn, dk, k_scale)
    if slab is None:
        out_specs = [pl.BlockSpec((bm, bn), lambda i, j: (row(i, j), col(i)))]
        out_shape = [jax.ShapeDtypeStruct((m, n), out_dtype)]
    else:
        out_specs = [pl.BlockSpec((bn // slab, bm, slab), lambda i, j: (col(i), row(i, j), 0))]
        out_shape = [jax.ShapeDtypeStruct((n // slab, m, slab), out_dtype)]
    scratch = [pltpu.VMEM((2, kd, bn), BF16)]
    if side is not None:
        c_col, ada_w, ada_b, first_col = side
        layers, da, na = ada_w.shape
        todo = layers * na - first_col
        sw = SIDE_COLS
        nside = todo // sw
        assert nside * sw == todo and first_col % sw == 0 and na % sw == 0 and nside <= nb * mb

        def sblk(i, j):
            return jnp.minimum(col(i) * mb + row(i, j), nside - 1)

        def flat(i, j):
            return first_col // sw + sblk(i, j)

        per_layer = na // sw
        in_specs += [pl.BlockSpec((da, 1), lambda i, j: (0, 0)),
                     pl.BlockSpec((None, da, sw), lambda i, j: (flat(i, j) // per_layer, 0, flat(i, j) % per_layer)),
                     pl.BlockSpec((None, 1, sw), lambda i, j: (flat(i, j) // per_layer, 0, flat(i, j) % per_layer))]
        args += [c_col, ada_w, ada_b.reshape(layers, 1, na)]
        out_specs.append(pl.BlockSpec((1, sw), lambda i, j: (0, sblk(i, j))))
        out_shape.append(jax.ShapeDtypeStruct((1, todo), F32))
        scratch.append(pltpu.VMEM((da, V7X_LANES), F32))
    out = pl.pallas_call(
        functools.partial(_mm1_kernel, nb=nb, ck=ck, act=act, has_bias=has_bias, rope=rope_static,
                          side=side is not None),
        grid=(nb + 1, mb),
        in_specs=in_specs,
        out_specs=out_specs,
        out_shape=out_shape,
        scratch_shapes=scratch,
        compiler_params=_params("arbitrary", "arbitrary"),
        name="matmul1",
    )(*args)
    return out if side is not None else out[0]


def _ret_kernel(lg_ref, q_ref, k_ref, v_ref, g_ref, gn_ref,
                o_ref, yacc_ref, st_ref, *, chunk, nsub):
    h, p, i = pl.program_id(0), pl.program_id(1), pl.program_id(2)
    nblk = pl.num_programs(2)
    blk_rows = chunk * nsub
    dk = q_ref.shape[1]
    lgf, lgb = lg_ref[0, h], lg_ref[1, h]

    @pl.when(i == 0)
    def _():
        st_ref[...] = jnp.zeros_like(st_ref)

    r = lax.broadcasted_iota(jnp.int32, (chunk, 1), 0).astype(F32)

    def wide(ref, rows):
        return jnp.concatenate([ref[0, rows, :], ref[1, rows, :]], axis=-1)

    def decay(expo):
        return jnp.broadcast_to(jnp.exp(expo), (chunk, dk)).astype(BF16)

    def state_update(k, kdec, v, lg):
        upd = lax.dot_general(k * kdec, v, (((0,), (0,)), ((), ())), preferred_element_type=F32)
        st_ref[...] = st_ref[...] * jnp.exp(jnp.full((1, 1), chunk * lg, F32)) + upd

    @pl.when(p == 0)
    def _():
        ri = lax.broadcasted_iota(jnp.int32, (chunk, chunk), 0)
        ci = lax.broadcasted_iota(jnp.int32, (chunk, chunk), 1)
        diff = (ri - ci).astype(F32)
        dmat = (jnp.where(diff >= 0, jnp.exp(jnp.maximum(diff, 0.0) * lgf), 0.0)
                + jnp.where(diff <= 0, jnp.exp(jnp.maximum(-diff, 0.0) * lgb), 0.0))
        qdec = decay((r + 1.0) * lgf)
        kdec = decay((chunk - 1.0 - r) * lgf)
        cdec = jnp.exp(jnp.full((1, 1), chunk * lgf, F32))
        for s in range(nsub):
            rows = pl.ds(s * chunk, chunk)
            q, k = q_ref[rows, :], k_ref[rows, :]
            sc = lax.dot_general(q, k, (((1,), (1,)), ((), ())), preferred_element_type=F32)
            scm = (sc * dmat).astype(BF16)
            qd, kd = q * qdec, k * kdec
            out_rows = pl.ds(pl.multiple_of(i * blk_rows + s * chunk, chunk), chunk)
            for hf in range(2):
                cols = slice(hf * dk, (hf + 1) * dk)
                v = v_ref[hf, rows, :]
                intra = jnp.dot(scm, v, preferred_element_type=F32)
                cross = jnp.dot(qd, st_ref[:, cols].astype(BF16), preferred_element_type=F32)
                yacc_ref[out_rows, cols] = intra + cross
                upd = lax.dot_general(kd, v, (((0,), (0,)), ((), ())), preferred_element_type=F32)
                st_ref[:, cols] = st_ref[:, cols] * cdec + upd

    @pl.when(p == 1)
    def _():
        qdec = decay((chunk - r) * lgb)
        kdec = decay(r * lgb)
        base = (nblk - 1 - i) * blk_rows
        for s in reversed(range(nsub)):
            rows = pl.ds(s * chunk, chunk)
            q, k, v = q_ref[rows, :], k_ref[rows, :], wide(v_ref, rows)
            cross = jnp.dot(q * qdec, st_ref[...].astype(BF16), preferred_element_type=F32)
            y = yacc_ref[pl.ds(pl.multiple_of(base + s * chunk, chunk), chunk), :] + cross
            mu = jnp.mean(y, axis=-1, keepdims=True)
            yc = y - mu
            var = jnp.mean(yc * yc, axis=-1, keepdims=True)
            yn = yc * lax.rsqrt(var + EPS) * gn_ref[...]
            g = wide(g_ref, rows).astype(F32)
            o_ref[rows, :] = (g * jax.nn.sigmoid(g) * yn).astype(o_ref.dtype)
            state_update(k, kdec, v, lgb)


def _retention(proj, lg, gn_g, heads):
    nslab, s, dk = proj.shape
    d = nslab * dk // 6
    dv = 2 * dk
    chunk = min(RET_CHUNK, s)
    blk = min(RET_BLOCK, s)
    nblk = s // blk

    def cidx(p, i):
        return i + p * (nblk - 1 - 2 * i)

    grid_spec = pltpu.PrefetchScalarGridSpec(
        num_scalar_prefetch=1,
        grid=(heads, 2, nblk),
        in_specs=[
            pl.BlockSpec((None, blk, dk), lambda h, p, i, lg: (h, cidx(p, i), 0)),
            pl.BlockSpec((None, blk, dk), lambda h, p, i, lg: (heads + h, cidx(p, i), 0)),
            pl.BlockSpec((2, blk, dk), lambda h, p, i, lg: (heads + h, cidx(p, i), 0)),
            pl.BlockSpec((2, blk, dk), lambda h, p, i, lg: (2 * heads + h, nblk - 1 - i * p, 0)),
            pl.BlockSpec((1, dv), lambda h, p, i, lg: (0, h)),
        ],
        out_specs=pl.BlockSpec((blk, dv), lambda h, p, i, lg: (nblk - 1 - i * p, h)),
        scratch_shapes=[pltpu.VMEM((s, dv), F32), pltpu.VMEM((dk, dv), F32)],
    )
    return pl.pallas_call(
        functools.partial(_ret_kernel, chunk=chunk, nsub=blk // chunk),
        grid_spec=grid_spec,
        out_shape=jax.ShapeDtypeStruct((s, 2 * d), BF16),
        compiler_params=_params("arbitrary", "arbitrary", "arbitrary"),
        name="retention",
    )(lg, proj, proj, proj, proj, gn_g)


def _dft_tables(n):
    j = np.arange(n)
    ang = 2.0 * np.pi * ((j[:, None] * j[None, :]) % n) / n
    return np.cos(ang), np.sin(ang)


def _fourier_stage1_kernel(h_ref, cs_ref, ma_ref, mb_ref, wr_ref, wi_ref, o_ref, *, tile):
    n1, jb, dg = h_ref.shape
    r = jnp.dot(h_ref[...].reshape(n1 * jb, dg), cs_ref[...], preferred_element_type=F32)
    r = r.astype(BF16).reshape(n1, jb, 2 * dg)
    half = n1 * tile
    for s in range(jb // tile):
        rs = r[:, s * tile:(s + 1) * tile, :].reshape(half, 2 * dg)
        t = (jnp.dot(ma_ref[...], rs[:, :dg], preferred_element_type=F32)
             + jnp.dot(mb_ref[...], rs[:, dg:], preferred_element_type=F32))
        tr, ti = t[:half], t[half:]
        wr, wi = wr_ref[s], wi_ref[s]
        rows = pl.ds(s * tile, tile)
        o_ref[0, :, rows, :] = (tr * wr - ti * wi).astype(o_ref.dtype).reshape(n1, tile, dg)
        o_ref[1, :, rows, :] = (tr * wi + ti * wr).astype(o_ref.dtype).reshape(n1, tile, dg)


def _fourier_stage2_kernel(t_ref, c_ref, s_ref, o_ref):
    n1 = t_ref.shape[1]
    res = [jnp.dot(c_ref[...], t_ref[0, kk], preferred_element_type=F32)
           + jnp.dot(s_ref[...], t_ref[1, kk], preferred_element_type=F32) for kk in range(n1)]
    o_ref[...] = jnp.swapaxes(jnp.stack(res), 0, 1).astype(o_ref.dtype)


def _fourier_real(h, groups):
    s, d = h.shape
    dg = d // groups
    n1 = DFT_N1
    n2 = s // n1
    tile = BF16_SUBLANE_TILE
    jb = min(DFT_STEP_ROWS, n2)

    cd, sd = _dft_tables(dg)
    cs = jnp.asarray(np.concatenate([cd, sd], axis=1), F32).astype(BF16)
    c1, s1 = _dft_tables(n1)
    eye = np.eye(tile)
    ma = jnp.asarray(np.kron(np.concatenate([c1, -s1], axis=0), eye), F32).astype(BF16)
    mb = jnp.asarray(np.kron(np.concatenate([-s1, -c1], axis=0), eye), F32).astype(BF16)
    tw = 2.0 * np.pi * (np.arange(n1)[None, :, None]
                        * (np.arange(n2 // tile)[:, None, None] * tile + np.arange(tile)[None, None, :])) / s
    tw = tw.reshape(n2 // tile, n1 * tile, 1)
    wr = jnp.asarray(np.cos(tw), F32)
    wi = jnp.asarray(-np.sin(tw), F32)
    tp = pl.pallas_call(
        functools.partial(_fourier_stage1_kernel, tile=tile),
        grid=(n2 // jb, groups),
        in_specs=[pl.BlockSpec((n1, jb, dg), lambda j, g: (0, j, g)),
                  pl.BlockSpec((dg, 2 * dg), lambda j, g: (0, 0)),
                  pl.BlockSpec((2 * n1 * tile, n1 * tile), lambda j, g: (0, 0)),
                  pl.BlockSpec((2 * n1 * tile, n1 * tile), lambda j, g: (0, 0)),
                  pl.BlockSpec((jb // tile, n1 * tile, 1), lambda j, g: (j, 0, 0)),
                  pl.BlockSpec((jb // tile, n1 * tile, 1), lambda j, g: (j, 0, 0))],
        out_specs=pl.BlockSpec((2, n1, jb, dg), lambda j, g: (0, 0, j, g)),
        out_shape=jax.ShapeDtypeStruct((2, n1, n2, d), BF16),
        compiler_params=_params("parallel", "parallel"),
        name="fourier_stage1",
    )(h.reshape(n1, n2, d), cs, ma, mb, wr, wi)

    c2, s2 = _dft_tables(n2)
    db = min(DFT_STEP_COLS, d)
    mixed = pl.pallas_call(
        _fourier_stage2_kernel,
        grid=(d // db,),
        in_specs=[pl.BlockSpec((2, n1, n2, db), lambda e: (0, 0, 0, e)),
                  pl.BlockSpec((n2, n2), lambda e: (0, 0)),
                  pl.BlockSpec((n2, n2), lambda e: (0, 0))],
        out_specs=pl.BlockSpec((n2, n1, db), lambda e: (0, 0, e)),
        out_shape=jax.ShapeDtypeStruct((n2, n1, d), BF16),
        compiler_params=_params("parallel"),
        name="fourier_stage2",
    )(tp, jnp.asarray(c2, F32).astype(BF16), jnp.asarray(s2, F32).astype(BF16))
    return mixed.reshape(s, d)


def kernel(x, c, ada_w, ada_b, norm_g, ret_w_in, ret_w_out, ret_gn_g, ret_decay_fwd,
           ret_decay_bwd, fno_w, fno_b, mlp_w1, mlp_w2):
    batch, s, d = x.shape
    assert batch == 1
    depth = ada_w.shape[0]
    heads, groups = RET_HEADS, FNO_GROUPS
    x = x.reshape(s, d)

    assert depth == 2
    c_col = c.reshape(d, 1)
    mods = [_adaln(c_col, ada_w, ada_b, 2 * d)[0], None]

    def mod(layer, idx):
        return mods[layer][:, idx * d:(idx + 1) * d]

    def gain(layer, idx):
        return norm_g[layer, idx].reshape(1, d)

    half = d // heads // 2
    inv = ROPE_BASE ** (-jnp.arange(half, dtype=F32) / half)
    ang = jnp.arange(s, dtype=F32)[:, None] * inv[None, :]
    cos, sin = jnp.cos(ang), jnp.sin(ang)

    (h,) = _row_call(x, gx=gain(0, 0), sc=mod(0, 1), sh=mod(0, 0))
    for layer in range(depth):
        occ = layer // 2
        if layer % 2 == 0:
            dk = d // heads
            proj, rest = _matmul1(h, ret_w_in, occ, rope=(cos, sin, dk, float(dk) ** -0.5),
                                  slab=dk, side=(c_col, ada_w, ada_b, 2 * d))
            mods = [jnp.concatenate([mods[0], rest[:, :4 * d]], axis=1), rest[:, 4 * d:]]
            lg = jnp.stack([jax.nn.log_sigmoid(ret_decay_fwd[occ].astype(F32)),
                            jax.nn.log_sigmoid(ret_decay_bwd[occ].astype(F32))])
            yh = _retention(proj, lg, ret_gn_g[occ].reshape(1, 2 * d), heads)
            y = _matmulk(yh, ret_w_out, occ)
        else:
            mixed = _fourier_real(h, groups)
            y = _matmul1(mixed, fno_w, occ, bias=fno_b[occ].reshape(1, d))
        x, h = _row_call(x, y=y, gate=mod(layer, 2), gy=gain(layer, 1),
                         gx=gain(layer, 2), sc=mod(layer, 4), sh=mod(layer, 3))
        a = _matmul1(h, mlp_w1, layer, act="relu2")
        y = _matmulk(a, mlp_w2, layer)
        if layer + 1 < depth:
            x, h = _row_call(x, y=y, gate=mod(layer, 5), gy=gain(layer, 3),
                             gx=gain(layer + 1, 0), sc=mod(layer + 1, 1), sh=mod(layer + 1, 0))
        else:
            (x,) = _row_call(x, y=y, gate=mod(layer, 5), gy=gain(layer, 3))
    return x.reshape(batch, s, d)
```

```python
import functools

import numpy as np
import jax
import jax.numpy as jnp
from jax import lax
from jax.experimental import pallas as pl
from jax.experimental.pallas import tpu as pltpu

N_MOD = 6
RET_IN_PARTS = 6
RET_HEADS = 16
FNO_GROUPS = 8
ROPE_BASE = 10000.0
EPS = 1e-6

V7X_VMEM_BYTES = 64 * 1024 * 1024
V7X_LANES = 128
BF16_SUBLANE_TILE = 16
VMEM_LIMIT_BYTES = V7X_VMEM_BYTES - 4 * 1024 * 1024

RET_CHUNK = 256
RET_BLOCK = 4096
DFT_N1 = 16
DFT_STEP_ROWS = 128
DFT_STEP_COLS = 256
ROW_BLOCK = 512
ADALN_COLS = 1024
MM1_BLOCK = (1024, 1024)
MMK_BLOCK = (2048, 1024, 2048)
MM1_ROW_SUB = 512
MMK_ROW_SUB = 512
SIDE_COLS = 256

F32 = jnp.float32
BF16 = jnp.bfloat16


def _params(*sem):
    return pltpu.CompilerParams(dimension_semantics=sem, vmem_limit_bytes=VMEM_LIMIT_BYTES)


def _adaln_kernel(c_ref, w_ref, b_ref, o_ref):
    c = c_ref[...]
    s = c * jax.nn.sigmoid(c)
    o_ref[0] = jnp.sum(w_ref[0] * s, axis=0, keepdims=True) + b_ref[0]


def _adaln(c_col, ada_w, ada_b, ncols):
    depth, d, n = ada_w.shape
    bn = min(ADALN_COLS, ncols)
    return pl.pallas_call(
        _adaln_kernel,
        grid=(1, ncols // bn),
        in_specs=[pl.BlockSpec((d, 1), lambda l, j: (0, 0)),
                  pl.BlockSpec((1, d, bn), lambda l, j: (l, 0, j)),
                  pl.BlockSpec((1, 1, bn), lambda l, j: (l, 0, j))],
        out_specs=pl.BlockSpec((1, 1, bn), lambda l, j: (l, 0, j)),
        out_shape=jax.ShapeDtypeStruct((1, 1, ncols), F32),
        compiler_params=_params("parallel", "parallel"),
        name="adaln",
    )(c_col, ada_w, ada_b.reshape(depth, 1, n))


def _rms(x, g):
    ms = jnp.mean(x * x, axis=-1, keepdims=True)
    return x * lax.rsqrt(ms + EPS) * g


def _row_kernel(*refs, has_y, has_h):
    refs = list(refs)
    x_ref = refs.pop(0)
    x = x_ref[...]
    if has_y:
        y_ref, gate_ref, gy_ref = refs.pop(0), refs.pop(0), refs.pop(0)
    if has_h:
        gx_ref, sc_ref, sh_ref = refs.pop(0), refs.pop(0), refs.pop(0)
    if has_y:
        xo_ref = refs.pop(0)
        x = x + gate_ref[...] * _rms(y_ref[...].astype(F32), gy_ref[...])
        xo_ref[...] = x
    if has_h:
        h_ref = refs.pop(0)
        h_ref[...] = (_rms(x, gx_ref[...]) * (1.0 + sc_ref[...]) + sh_ref[...]).astype(BF16)


def _row_call(x, y=None, gate=None, gy=None, gx=None, sc=None, sh=None):
    s, d = x.shape
    bm = min(ROW_BLOCK, s)
    has_y, has_h = y is not None, gx is not None
    row = pl.BlockSpec((bm, d), lambda i: (i, 0))
    vec = pl.BlockSpec((1, d), lambda i: (0, 0))
    args, in_specs, out_specs, out_shape = [x], [row], [], []
    if has_y:
        args += [y, gate, gy]
        in_specs += [row, vec, vec]
        out_specs.append(row)
        out_shape.append(jax.ShapeDtypeStruct((s, d), F32))
    if has_h:
        args += [gx, sc, sh]
        in_specs += [vec, vec, vec]
        out_specs.append(row)
        out_shape.append(jax.ShapeDtypeStruct((s, d), BF16))
    out = pl.pallas_call(
        functools.partial(_row_kernel, has_y=has_y, has_h=has_h),
        grid=(s // bm,),
        in_specs=in_specs, out_specs=out_specs, out_shape=out_shape,
        compiler_params=_params("parallel"),
        name="row_norm",
    )(*args)
    return out


def _mmk_kernel(a_ref, w_ref, o_ref, wb_ref, acc_ref, *, nk, ck, bm):
    b, m = pl.program_id(0), pl.program_id(1)
    k = (b - 1) % nk
    rb = min(MMK_ROW_SUB, bm)

    def cast_next():
        wb_ref[b % 2, pl.ds(pl.multiple_of(m * ck, ck), ck), :] = w_ref[...].astype(BF16)

    def body(first, last):
        cast_next()
        wb = wb_ref.at[(b + 1) % 2]
        for r in range(bm // rb):
            part = jnp.dot(a_ref[r * rb:(r + 1) * rb, :], wb[...], preferred_element_type=F32)
            rows = pl.ds(pl.multiple_of(m * bm + r * rb, rb), rb)
            if not first:
                part = acc_ref[rows, :] + part
            if last:
                o_ref[r * rb:(r + 1) * rb, :] = part.astype(o_ref.dtype)
            else:
                acc_ref[rows, :] = part

    @pl.when(b == 0)
    def _():
        cast_next()

    @pl.when(jnp.logical_and(b > 0, k == 0))
    def _():
        body(True, False)

    @pl.when(jnp.logical_and(b > 0, jnp.logical_and(k > 0, k < nk - 1)))
    def _():
        body(False, False)

    @pl.when(jnp.logical_and(b > 0, k == nk - 1))
    def _():
        body(False, True)


def _matmulk(a, w, layer, halves=2):
    m, kd = a.shape
    _, _, n = w.shape
    bm, bn, bk = MMK_BLOCK
    nb, nk = n // bn, kd // bk
    mb = m // halves // bm
    ck = bk // mb
    nblocks = halves * nb * nk
    assert nk > 1 and mb * bm * halves == m and ck * mb == bk

    def dec(x):
        return x // (nk * nb), (x // nk) % nb, x % nk

    def a_map(b, j):
        hh, _, kk = dec(jnp.maximum(b - 1, 0))
        return hh * mb + jnp.where(b > 0, j, 0), kk

    def w_map(b, j):
        _, nn, kk = dec(jnp.minimum(b, nblocks - 1))
        return layer, kk * mb + j, nn

    def o_map(b, j):
        hh, nn, kk = dec(jnp.maximum(b - 1, 0))
        return hh * mb + jnp.where(jnp.logical_and(b > 0, kk == nk - 1), j, 0), nn

    return pl.pallas_call(
        functools.partial(_mmk_kernel, nk=nk, ck=ck, bm=bm),
        grid=(nblocks + 1, mb),
        in_specs=[pl.BlockSpec((bm, bk), a_map),
                  pl.BlockSpec((None, ck, bn), w_map)],
        out_specs=pl.BlockSpec((bm, bn), o_map),
        out_shape=jax.ShapeDtypeStruct((m, n), BF16),
        scratch_shapes=[pltpu.VMEM((2, bk, bn), BF16), pltpu.VMEM((m // halves, bn), F32)],
        compiler_params=_params("arbitrary", "arbitrary"),
        name="matmulk",
    )(a, w)


def _mm1_kernel(*refs, nb, ck, act, has_bias, rope, side):
    refs = list(refs)
    a_ref, w_ref = refs.pop(0), refs.pop(0)
    b_ref = refs.pop(0) if has_bias else None
    cos_ref, sin_ref = (refs.pop(0), refs.pop(0)) if rope else (None, None)
    if side:
        c_ref, aw_ref, ab_ref = refs.pop(0), refs.pop(0), refs.pop(0)
        o_ref, mo_ref, wb_ref, sb_ref = refs
    else:
        o_ref, wb_ref = refs
    n, m = pl.program_id(0), pl.program_id(1)

    def cast_next():
        wb_ref[n % 2, pl.ds(pl.multiple_of(m * ck, ck), ck), :] = w_ref[...].astype(BF16)

    def epilogue(acc):
        if has_bias:
            acc = acc + b_ref[...]
        if act == "relu2":
            acc = jnp.maximum(acc, 0.0)
            acc = acc * acc
        return acc

    def compute(rotary):
        cast_next()
        if side:
            lanes = sb_ref.shape[1]
            mv = [jnp.sum(aw_ref[:, t * lanes:(t + 1) * lanes] * sb_ref[...], axis=0, keepdims=True)
                  for t in range(aw_ref.shape[1] // lanes)]
            mo_ref[...] = jnp.concatenate(mv, axis=-1) + ab_ref[...]
        bm = a_ref.shape[0]
        rb = min(MM1_ROW_SUB, bm)
        wb = wb_ref.at[(n + 1) % 2]
        if rotary:
            nq, dk, k_scale = rope
            scale = jnp.where(n - 1 >= nq, k_scale, 1.0)
        for r in range(bm // rb):
            rows = slice(r * rb, (r + 1) * rb)
            acc = epilogue(jnp.dot(a_ref[rows, :], wb[...], preferred_element_type=F32))
            if rotary:
                c, s = cos_ref[rows, :] * scale, sin_ref[rows, :] * scale
                half = dk // 2
                parts = []
                for hh in range(acc.shape[1] // dk):
                    x1 = acc[:, hh * dk:hh * dk + half]
                    x2 = acc[:, hh * dk + half:(hh + 1) * dk]
                    parts += [x1 * c - x2 * s, x1 * s + x2 * c]
                acc = jnp.concatenate(parts, axis=-1)
            if len(o_ref.shape) == 3:
                slab = o_ref.shape[2]
                for hh in range(o_ref.shape[0]):
                    o_ref[hh, rows, :] = acc[:, hh * slab:(hh + 1) * slab].astype(o_ref.dtype)
            else:
                o_ref[rows, :] = acc.astype(o_ref.dtype)

    @pl.when(n == 0)
    def _():
        cast_next()

    if side:
        @pl.when(jnp.logical_and(n == 0, m == 0))
        def _():
            c = c_ref[...]
            sb_ref[...] = jnp.broadcast_to(c * jax.nn.sigmoid(c), sb_ref.shape)

    if rope:
        nqk = 2 * rope[0]

        @pl.when(jnp.logical_and(n > 0, n - 1 < nqk))
        def _():
            compute(True)

        @pl.when(n - 1 >= nqk)
        def _():
            compute(False)
    else:
        @pl.when(n > 0)
        def _():
            compute(False)


def _matmul1(a, w, layer, bias=None, act=None, out_dtype=BF16, rope=None, slab=None, side=None):
    m, kd = a.shape
    _, _, n = w.shape
    bm, bn = min(MM1_BLOCK[0], m), min(MM1_BLOCK[1], n)
    nb, mb = n // bn, m // bm
    ck = kd // mb
    has_bias = bias is not None

    def row(i, j):
        return jnp.where(i > 0, j, 0)

    def col(i):
        return jnp.maximum(i - 1, 0)

    in_specs = [pl.BlockSpec((bm, kd), lambda i, j: (row(i, j), 0)),
                pl.BlockSpec((None, ck, bn), lambda i, j: (layer, j, jnp.minimum(i, nb - 1)))]
    args = [a, w]
    if has_bias:
        in_specs.append(pl.BlockSpec((1, bn), lambda i, j: (0, col(i))))
        args.append(bias)
    rope_static = None
    if rope is not None:
        cos, sin, dk, k_scale = rope
        in_specs += [pl.BlockSpec((bm, dk // 2), lambda i, j: (row(i, j), 0))] * 2
        args += [cos, sin]
        rope_static = (n // RET_IN_PARTS // bn, dk, k_scale)
    if slab is None:
        out_specs = [pl.BlockSpec((bm, bn), lambda i, j: (row(i, j), col(i)))]
        out_shape = [jax.ShapeDtypeStruct((m, n), out_dtype)]
    else:
        out_specs = [pl.BlockSpec((bn // slab, bm, slab), lambda i, j: (col(i), row(i, j), 0))]
        out_shape = [jax.ShapeDtypeStruct((n // slab, m, slab), out_dtype)]
    scratch = [pltpu.VMEM((2, kd, bn), BF16)]
    if side is not None:
        c_col, ada_w, ada_b, first_col = side
        layers, da, na = ada_w.shape
        todo = layers * na - first_col
        sw = SIDE_COLS
        nside = todo // sw
        assert nside * sw == todo and first_col % sw == 0 and na % sw == 0 and nside <= nb * mb

        def sblk(i, j):
            return jnp.minimum(col(i) * mb + row(i, j), nside - 1)

        def flat(i, j):
            return first_col // sw + sblk(i, j)

        per_layer = na // sw
        in_specs += [pl.BlockSpec((da, 1), lambda i, j: (0, 0)),
                     pl.BlockSpec((None, da, sw), lambda i, j: (flat(i, j) // per_layer, 0, flat(i, j) % per_layer)),
                     pl.BlockSpec((None, 1, sw), lambda i, j: (flat(i, j) // per_layer, 0, flat(i, j) % per_layer))]
        args += [c_col, ada_w, ada_b.reshape(layers, 1, na)]
        out_specs.append(pl.BlockSpec((1, sw), lambda i, j: (0, sblk(i, j))))
        out_shape.append(jax.ShapeDtypeStruct((1, todo), F32))
        scratch.append(pltpu.VMEM((da, V7X_LANES), F32))
    out = pl.pallas_call(
        functools.partial(_mm1_kernel, nb=nb, ck=ck, act=act, has_bias=has_bias, rope=rope_static,
                          side=side is not None),
        grid=(nb + 1, mb),
        in_specs=in_specs,
        out_specs=out_specs,
        out_shape=out_shape,
        scratch_shapes=scratch,
        compiler_params=_params("arbitrary", "arbitrary"),
        name="matmul1",
    )(*args)
    return out if side is not None else out[0]


def _ret_kernel(lg_ref, q_ref, k_ref, v_ref, g_ref, gn_ref,
                o_ref, yacc_ref, st_ref, *, chunk, nsub):
    h, p, i = pl.program_id(0), pl.program_id(1), pl.program_id(2)
    nblk = pl.num_programs(2)
    blk_rows = chunk * nsub
    dk = q_ref.shape[1]
    lgf, lgb = lg_ref[0, h], lg_ref[1, h]

    @pl.when(i == 0)
    def _():
        st_ref[...] = jnp.zeros_like(st_ref)

    r = lax.broadcasted_iota(jnp.int32, (chunk, 1), 0).astype(F32)

    def wide(ref, rows):
        return jnp.concatenate([ref[0, rows, :], ref[1, rows, :]], axis=-1)

    def decay(expo):
        return jnp.broadcast_to(jnp.exp(expo), (chunk, dk)).astype(BF16)

    def state_update(k, kdec, v, lg):
        upd = lax.dot_general(k * kdec, v, (((0,), (0,)), ((), ())), preferred_element_type=F32)
        st_ref[...] = st_ref[...] * jnp.exp(jnp.full((1, 1), chunk * lg, F32)) + upd

    @pl.when(p == 0)
    def _():
        ri = lax.broadcasted_iota(jnp.int32, (chunk, chunk), 0)
        ci = lax.broadcasted_iota(jnp.int32, (chunk, chunk), 1)
        diff = (ri - ci).astype(F32)
        dmat = (jnp.where(diff >= 0, jnp.exp(jnp.maximum(diff, 0.0) * lgf), 0.0)
                + jnp.where(diff <= 0, jnp.exp(jnp.maximum(-diff, 0.0) * lgb), 0.0))
        qdec = decay((r + 1.0) * lgf)
        kdec = decay((chunk - 1.0 - r) * lgf)
        cdec = jnp.exp(jnp.full((1, 1), chunk * lgf, F32))
        for s in range(nsub):
            rows = pl.ds(s * chunk, chunk)
            q, k = q_ref[rows, :], k_ref[rows, :]
            sc = lax.dot_general(q, k, (((1,), (1,)), ((), ())), preferred_element_type=F32)
            scm = (sc * dmat).astype(BF16)
            qd, kd = q * qdec, k * kdec
            out_rows = pl.ds(pl.multiple_of(i * blk_rows + s * chunk, chunk), chunk)
            for hf in range(2):
                cols = slice(hf * dk, (hf + 1) * dk)
                v = v_ref[hf, rows, :]
                intra = jnp.dot(scm, v, preferred_element_type=F32)
                cross = jnp.dot(qd, st_ref[:, cols].astype(BF16), preferred_element_type=F32)
                yacc_ref[out_rows, cols] = intra + cross
                upd = lax.dot_general(kd, v, (((0,), (0,)), ((), ())), preferred_element_type=F32)
                st_ref[:, cols] = st_ref[:, cols] * cdec + upd

    @pl.when(p == 1)
    def _():
        qdec = decay((chunk - r) * lgb)
        kdec = decay(r * lgb)
        base = (nblk - 1 - i) * blk_rows
        for s in reversed(range(nsub)):
            rows = pl.ds(s * chunk, chunk)
            q, k, v = q_ref[rows, :], k_ref[rows, :], wide(v_ref, rows)
            cross = jnp.dot(q * qdec, st_ref[...].astype(BF16), preferred_element_type=F32)
            y = yacc_ref[pl.ds(pl.multiple_of(base + s * chunk, chunk), chunk), :] + cross
            mu = jnp.mean(y, axis=-1, keepdims=True)
            yc = y - mu
            var = jnp.mean(yc * yc, axis=-1, keepdims=True)
            yn = yc * lax.rsqrt(var + EPS) * gn_ref[...]
            g = wide(g_ref, rows).astype(F32)
            o_ref[rows, :] = (g * jax.nn.sigmoid(g) * yn).astype(o_ref.dtype)
            state_update(k, kdec, v, lgb)


def _retention(proj, lg, gn_g, heads):
    nslab, s, dk = proj.shape
    d = nslab * dk // 6
    dv = 2 * dk
    chunk = min(RET_CHUNK, s)
    blk = min(RET_BLOCK, s)
    nblk = s // blk

    def cidx(p, i):
        return i + p * (nblk - 1 - 2 * i)

    grid_spec = pltpu.PrefetchScalarGridSpec(
        num_scalar_prefetch=1,
        grid=(heads, 2, nblk),
        in_specs=[
            pl.BlockSpec((None, blk, dk), lambda h, p, i, lg: (h, cidx(p, i), 0)),
            pl.BlockSpec((None, blk, dk), lambda h, p, i, lg: (heads + h, cidx(p, i), 0)),
            pl.BlockSpec((2, blk, dk), lambda h, p, i, lg: (heads + h, cidx(p, i), 0)),
            pl.BlockSpec((2, blk, dk), lambda h, p, i, lg: (2 * heads + h, nblk - 1 - i * p, 0)),
            pl.BlockSpec((1, dv), lambda h, p, i, lg: (0, h)),
        ],
        out_specs=pl.BlockSpec((blk, dv), lambda h, p, i, lg: (nblk - 1 - i * p, h)),
        scratch_shapes=[pltpu.VMEM((s, dv), F32), pltpu.VMEM((dk, dv), F32)],
    )
    return pl.pallas_call(
        functools.partial(_ret_kernel, chunk=chunk, nsub=blk // chunk),
        grid_spec=grid_spec,
        out_shape=jax.ShapeDtypeStruct((s, 2 * d), BF16),
        compiler_params=_params("arbitrary", "arbitrary", "arbitrary"),
        name="retention",
    )(lg, proj, proj, proj, proj, gn_g)


def _dft_tables(n):
    j = np.arange(n)
    ang = 2.0 * np.pi * ((j[:, None] * j[None, :]) % n) / n
    return np.cos(ang), np.sin(ang)


def _fourier_stage1_kernel(h_ref, cs_ref, ma_ref, mb_ref, wr_ref, wi_ref, o_ref, *, tile):
    n1, jb, dg = h_ref.shape
    r = jnp.dot(h_ref[...].reshape(n1 * jb, dg), cs_ref[...], preferred_element_type=F32)
    r = r.astype(BF16).reshape(n1, jb, 2 * dg)
    half = n1 * tile
    for s in range(jb // tile):
        rs = r[:, s * tile:(s + 1) * tile, :].reshape(half, 2 * dg)
        t = (jnp.dot(ma_ref[...], rs[:, :dg], preferred_element_type=F32)
             + jnp.dot(mb_ref[...], rs[:, dg:], preferred_element_type=F32))
        tr, ti = t[:half], t[half:]
        wr, wi = wr_ref[s], wi_ref[s]
        rows = pl.ds(s * tile, tile)
        o_ref[0, :, rows, :] = (tr * wr - ti * wi).astype(o_ref.dtype).reshape(n1, tile, dg)
        o_ref[1, :, rows, :] = (tr * wi + ti * wr).astype(o_ref.dtype).reshape(n1, tile, dg)


def _fourier_stage2_kernel(t_ref, c_ref, s_ref, o_ref):
    n1 = t_ref.shape[1]
    res = [jnp.dot(c_ref[...], t_ref[0, kk], preferred_element_type=F32)
           + jnp.dot(s_ref[...], t_ref[1, kk], preferred_element_type=F32) for kk in range(n1)]
    o_ref[...] = jnp.swapaxes(jnp.stack(res), 0, 1).astype(o_ref.dtype)


def _fourier_real(h, groups):
    s, d = h.shape
    dg = d // groups
    n1 = DFT_N1
    n2 = s // n1
    tile = BF16_SUBLANE_TILE
    jb = min(DFT_STEP_ROWS, n2)

    cd, sd = _dft_tables(dg)
    cs = jnp.asarray(np.concatenate([cd, sd], axis=1), F32).astype(BF16)
    c1, s1 = _dft_tables(n1)
    eye = np.eye(tile)
    ma = jnp.asarray(np.kron(np.concatenate([c1, -s1], axis=0), eye), F32).astype(BF16)
    mb = jnp.asarray(np.kron(np.concatenate([-s1, -c1], axis=0), eye), F32).astype(BF16)
    tw = 2.0 * np.pi * (np.arange(n1)[None, :, None]
                        * (np.arange(n2 // tile)[:, None, None] * tile + np.arange(tile)[None, None, :])) / s
    tw = tw.reshape(n2 // tile, n1 * tile, 1)
    wr = jnp.asarray(np.cos(tw), F32)
    wi = jnp.asarray(-np.sin(tw), F32)
    tp = pl.pallas_call(
        functools.partial(_fourier_stage1_kernel, tile=tile),
        grid=(n2 // jb, groups),
        in_specs=[pl.BlockSpec((n1, jb, dg), lambda j, g: (0, j, g)),
                  pl.BlockSpec((dg, 2 * dg), lambda j, g: (0, 0)),
                  pl.BlockSpec((2 * n1 * tile, n1 * tile), lambda j, g: (0, 0)),
                  pl.BlockSpec((2 * n1 * tile, n1 * tile), lambda j, g: (0, 0)),
                  pl.BlockSpec((jb // tile, n1 * tile, 1), lambda j, g: (j, 0, 0)),
                  pl.BlockSpec((jb // tile, n1 * tile, 1), lambda j, g: (j, 0, 0))],
        out_specs=pl.BlockSpec((2, n1, jb, dg), lambda j, g: (0, 0, j, g)),
        out_shape=jax.ShapeDtypeStruct((2, n1, n2, d), BF16),
        compiler_params=_params("parallel", "parallel"),
        name="fourier_stage1",
    )(h.reshape(n1, n2, d), cs, ma, mb, wr, wi)

    c2, s2 = _dft_tables(n2)
    db = min(DFT_STEP_COLS, d)
    mixed = pl.pallas_call(
        _fourier_stage2_kernel,
        grid=(d // db,),
        in_specs=[pl.BlockSpec((2, n1, n2, db), lambda e: (0, 0, 0, e)),
                  pl.BlockSpec((n2, n2), lambda e: (0, 0)),
                  pl.BlockSpec((n2, n2), lambda e: (0, 0))],
        out_specs=pl.BlockSpec((n2, n1, db), lambda e: (0, 0, e)),
        out_shape=jax.ShapeDtypeStruct((n2, n1, d), BF16),
        compiler_params=_params("parallel"),
        name="fourier_stage2",
    )(tp, jnp.asarray(c2, F32).astype(BF16), jnp.asarray(s2, F32).astype(BF16))
    return mixed.reshape(s, d)


def kernel(x, c, ada_w, ada_b, norm_g, ret_w_in, ret_w_out, ret_gn_g, ret_decay_fwd,
           ret_decay_bwd, fno_w, fno_b, mlp_w1, mlp_w2):
    batch, s, d = x.shape
    assert batch == 1
    depth = ada_w.shape[0]
    heads, groups = RET_HEADS, FNO_GROUPS
    x = x.reshape(s, d)

    assert depth == 2 and ada_w.shape[2] == N_MOD * d
    c_col = c.reshape(d, 1)
    mods = [_adaln(c_col, ada_w, ada_b, 2 * d)[0], None]

    def mod(layer, idx):
        return mods[layer][:, idx * d:(idx + 1) * d]

    def gain(layer, idx):
        return norm_g[layer, idx].reshape(1, d)

    half = d // heads // 2
    inv = ROPE_BASE ** (-jnp.arange(half, dtype=F32) / half)
    ang = jnp.arange(s, dtype=F32)[:, None] * inv[None, :]
    cos, sin = jnp.cos(ang), jnp.sin(ang)

    (h,) = _row_call(x, gx=gain(0, 0), sc=mod(0, 1), sh=mod(0, 0))
    for layer in range(depth):
        occ = layer // 2
        if layer % 2 == 0:
            dk = d // heads
            proj, rest = _matmul1(h, ret_w_in, occ, rope=(cos, sin, dk, float(dk) ** -0.5),
                                  slab=dk, side=(c_col, ada_w, ada_b, 2 * d))
            mods = [jnp.concatenate([mods[0], rest[:, :4 * d]], axis=1), rest[:, 4 * d:]]
            lg = jnp.stack([jax.nn.log_sigmoid(ret_decay_fwd[occ].astype(F32)),
                            jax.nn.log_sigmoid(ret_decay_bwd[occ].astype(F32))])
            yh = _retention(proj, lg, ret_gn_g[occ].reshape(1, 2 * d), heads)
            y = _matmulk(yh, ret_w_out, occ)
        else:
            mixed = _fourier_real(h, groups)
            y = _matmul1(mixed, fno_w, occ, bias=fno_b[occ].reshape(1, d))
        x, h = _row_call(x, y=y, gate=mod(layer, 2), gy=gain(layer, 1),
                         gx=gain(layer, 2), sc=mod(layer, 4), sh=mod(layer, 3))
        a = _matmul1(h, mlp_w1, layer, act="relu2")
        y = _matmulk(a, mlp_w2, layer)
        if layer + 1 < depth:
            x, h = _row_call(x, y=y, gate=mod(layer, 5), gy=gain(layer, 3),
                             gx=gain(layer + 1, 0), sc=mod(layer + 1, 1), sh=mod(layer + 1, 0))
        else:
            (x,) = _row_call(x, y=y, gate=mod(layer, 5), gy=gain(layer, 3))
    return x.reshape(batch, s, d)
```

```python
import functools

import numpy as np
import jax
import jax.numpy as jnp
from jax import lax
from jax.experimental import pallas as pl
from jax.experimental.pallas import tpu as pltpu

N_MOD = 6
RET_IN_PARTS = 6
RET_HEADS = 16
FNO_GROUPS = 8
ROPE_BASE = 10000.0
EPS = 1e-6

V7X_VMEM_BYTES = 64 * 1024 * 1024
V7X_LANES = 128
BF16_SUBLANE_TILE = 16
VMEM_LIMIT_BYTES = V7X_VMEM_BYTES - 4 * 1024 * 1024

RET_CHUNK = 256
RET_BLOCK = 4096
DFT_N1 = 16
DFT_STEP_ROWS = 256
DFT_STEP_COLS = 256
ROW_BLOCK = 512
ADALN_COLS = 1024
MM1_BLOCK = (1024, 1024)
MMK_BLOCK = (2048, 1024, 2048)
MM1_ROW_SUB = 512
MMK_ROW_SUB = 512
SIDE_COLS = 256

F32 = jnp.float32
BF16 = jnp.bfloat16


def _params(*sem):
    return pltpu.CompilerParams(dimension_semantics=sem, vmem_limit_bytes=VMEM_LIMIT_BYTES)


def _adaln_kernel(c_ref, w_ref, b_ref, o_ref):
    c = c_ref[...]
    s = c * jax.nn.sigmoid(c)
    o_ref[0] = jnp.sum(w_ref[0] * s, axis=0, keepdims=True) + b_ref[0]


def _adaln(c_col, ada_w, ada_b, ncols):
    depth, d, n = ada_w.shape
    bn = min(ADALN_COLS, ncols)
    return pl.pallas_call(
        _adaln_kernel,
        grid=(1, ncols // bn),
        in_specs=[pl.BlockSpec((d, 1), lambda l, j: (0, 0)),
                  pl.BlockSpec((1, d, bn), lambda l, j: (l, 0, j)),
                  pl.BlockSpec((1, 1, bn), lambda l, j: (l, 0, j))],
        out_specs=pl.BlockSpec((1, 1, bn), lambda l, j: (l, 0, j)),
        out_shape=jax.ShapeDtypeStruct((1, 1, ncols), F32),
        compiler_params=_params("parallel", "parallel"),
        name="adaln",
    )(c_col, ada_w, ada_b.reshape(depth, 1, n))


def _rms(x, g):
    ms = jnp.mean(x * x, axis=-1, keepdims=True)
    return x * lax.rsqrt(ms + EPS) * g


def _row_kernel(*refs, has_y, has_h):
    refs = list(refs)
    x_ref = refs.pop(0)
    x = x_ref[...]
    if has_y:
        y_ref, gate_ref, gy_ref = refs.pop(0), refs.pop(0), refs.pop(0)
    if has_h:
        gx_ref, sc_ref, sh_ref = refs.pop(0), refs.pop(0), refs.pop(0)
    if has_y:
        xo_ref = refs.pop(0)
        x = x + gate_ref[...] * _rms(y_ref[...].astype(F32), gy_ref[...])
        xo_ref[...] = x
    if has_h:
        h_ref = refs.pop(0)
        h_ref[...] = (_rms(x, gx_ref[...]) * (1.0 + sc_ref[...]) + sh_ref[...]).astype(BF16)


def _row_call(x, y=None, gate=None, gy=None, gx=None, sc=None, sh=None):
    s, d = x.shape
    bm = min(ROW_BLOCK, s)
    has_y, has_h = y is not None, gx is not None
    row = pl.BlockSpec((bm, d), lambda i: (i, 0))
    vec = pl.BlockSpec((1, d), lambda i: (0, 0))
    args, in_specs, out_specs, out_shape = [x], [row], [], []
    if has_y:
        args += [y, gate, gy]
        in_specs += [row, vec, vec]
        out_specs.append(row)
        out_shape.append(jax.ShapeDtypeStruct((s, d), F32))
    if has_h:
        args += [gx, sc, sh]
        in_specs += [vec, vec, vec]
        out_specs.append(row)
        out_shape.append(jax.ShapeDtypeStruct((s, d), BF16))
    out = pl.pallas_call(
        functools.partial(_row_kernel, has_y=has_y, has_h=has_h),
        grid=(s // bm,),
        in_specs=in_specs, out_specs=out_specs, out_shape=out_shape,
        compiler_params=_params("parallel"),
        name="row_norm",
    )(*args)
    return out


def _mmk_kernel(a_ref, w_ref, o_ref, wb_ref, acc_ref, *, nk, ck, bm):
    b, m = pl.program_id(0), pl.program_id(1)
    k = (b - 1) % nk
    rb = min(MMK_ROW_SUB, bm)

    def cast_next():
        wb_ref[b % 2, pl.ds(pl.multiple_of(m * ck, ck), ck), :] = w_ref[...].astype(BF16)

    def body(first, last):
        cast_next()
        wb = wb_ref.at[(b + 1) % 2]
        for r in range(bm // rb):
            part = jnp.dot(a_ref[r * rb:(r + 1) * rb, :], wb[...], preferred_element_type=F32)
            rows = pl.ds(pl.multiple_of(m * bm + r * rb, rb), rb)
            if not first:
                part = acc_ref[rows, :] + part
            if last:
                o_ref[r * rb:(r + 1) * rb, :] = part.astype(o_ref.dtype)
            else:
                acc_ref[rows, :] = part

    @pl.when(b == 0)
    def _():
        cast_next()

    @pl.when(jnp.logical_and(b > 0, k == 0))
    def _():
        body(True, False)

    @pl.when(jnp.logical_and(b > 0, jnp.logical_and(k > 0, k < nk - 1)))
    def _():
        body(False, False)

    @pl.when(jnp.logical_and(b > 0, k == nk - 1))
    def _():
        body(False, True)


def _matmulk(a, w, layer, halves=2):
    m, kd = a.shape
    _, _, n = w.shape
    bm, bn, bk = MMK_BLOCK
    nb, nk = n // bn, kd // bk
    mb = m // halves // bm
    ck = bk // mb
    nblocks = halves * nb * nk
    assert nk > 1 and mb * bm * halves == m and ck * mb == bk

    def dec(x):
        return x // (nk * nb), (x // nk) % nb, x % nk

    def a_map(b, j):
        hh, _, kk = dec(jnp.maximum(b - 1, 0))
        return hh * mb + jnp.where(b > 0, j, 0), kk

    def w_map(b, j):
        _, nn, kk = dec(jnp.minimum(b, nblocks - 1))
        return layer, kk * mb + j, nn

    def o_map(b, j):
        hh, nn, kk = dec(jnp.maximum(b - 1, 0))
        return hh * mb + jnp.where(jnp.logical_and(b > 0, kk == nk - 1), j, 0), nn

    return pl.pallas_call(
        functools.partial(_mmk_kernel, nk=nk, ck=ck, bm=bm),
        grid=(nblocks + 1, mb),
        in_specs=[pl.BlockSpec((bm, bk), a_map),
                  pl.BlockSpec((None, ck, bn), w_map)],
        out_specs=pl.BlockSpec((bm, bn), o_map),
        out_shape=jax.ShapeDtypeStruct((m, n), BF16),
        scratch_shapes=[pltpu.VMEM((2, bk, bn), BF16), pltpu.VMEM((m // halves, bn), F32)],
        compiler_params=_params("arbitrary", "arbitrary"),
        name="matmulk",
    )(a, w)


def _mm1_kernel(*refs, nb, ck, act, has_bias, rope, side):
    refs = list(refs)
    a_ref, w_ref = refs.pop(0), refs.pop(0)
    b_ref = refs.pop(0) if has_bias else None
    cos_ref, sin_ref = (refs.pop(0), refs.pop(0)) if rope else (None, None)
    if side:
        c_ref, aw_ref, ab_ref = refs.pop(0), refs.pop(0), refs.pop(0)
        o_ref, mo_ref, wb_ref, sb_ref = refs
    else:
        o_ref, wb_ref = refs
    n, m = pl.program_id(0), pl.program_id(1)

    def cast_next():
        wb_ref[n % 2, pl.ds(pl.multiple_of(m * ck, ck), ck), :] = w_ref[...].astype(BF16)

    def epilogue(acc):
        if has_bias:
            acc = acc + b_ref[...]
        if act == "relu2":
            acc = jnp.maximum(acc, 0.0)
            acc = acc * acc
        return acc

    def compute(rotary):
        cast_next()
        if side:
            lanes = sb_ref.shape[1]
            mv = [jnp.sum(aw_ref[:, t * lanes:(t + 1) * lanes] * sb_ref[...], axis=0, keepdims=True)
                  for t in range(aw_ref.shape[1] // lanes)]
            mo_ref[...] = jnp.concatenate(mv, axis=-1) + ab_ref[...]
        bm = a_ref.shape[0]
        rb = min(MM1_ROW_SUB, bm)
        wb = wb_ref.at[(n + 1) % 2]
        if rotary:
            nq, dk, k_scale = rope
            scale = jnp.where(n - 1 >= nq, k_scale, 1.0)
        for r in range(bm // rb):
            rows = slice(r * rb, (r + 1) * rb)
            acc = epilogue(jnp.dot(a_ref[rows, :], wb[...], preferred_element_type=F32))
            if rotary:
                c, s = cos_ref[rows, :] * scale, sin_ref[rows, :] * scale
                half = dk // 2
                parts = []
                for hh in range(acc.shape[1] // dk):
                    x1 = acc[:, hh * dk:hh * dk + half]
                    x2 = acc[:, hh * dk + half:(hh + 1) * dk]
                    parts += [x1 * c - x2 * s, x1 * s + x2 * c]
                acc = jnp.concatenate(parts, axis=-1)
            if len(o_ref.shape) == 3:
                slab = o_ref.shape[2]
                for hh in range(o_ref.shape[0]):
                    o_ref[hh, rows, :] = acc[:, hh * slab:(hh + 1) * slab].astype(o_ref.dtype)
            else:
                o_ref[rows, :] = acc.astype(o_ref.dtype)

    @pl.when(n == 0)
    def _():
        cast_next()

    if side:
        @pl.when(jnp.logical_and(n == 0, m == 0))
        def _():
            c = c_ref[...]
            sb_ref[...] = jnp.broadcast_to(c * jax.nn.sigmoid(c), sb_ref.shape)

    if rope:
        nqk = 2 * rope[0]

        @pl.when(jnp.logical_and(n > 0, n - 1 < nqk))
        def _():
            compute(True)

        @pl.when(n - 1 >= nqk)
        def _():
            compute(False)
    else:
        @pl.when(n > 0)
        def _():
            compute(False)


def _matmul1(a, w, layer, bias=None, act=None, out_dtype=BF16, rope=None, slab=None, side=None):
    m, kd = a.shape
    _, _, n = w.shape
    bm, bn = min(MM1_BLOCK[0], m), min(MM1_BLOCK[1], n)
    nb, mb = n // bn, m // bm
    ck = kd // mb
    has_bias = bias is not None

    def row(i, j):
        return jnp.where(i > 0, j, 0)

    def col(i):
        return jnp.maximum(i - 1, 0)

    in_specs = [pl.BlockSpec((bm, kd), lambda i, j: (row(i, j), 0)),
                pl.BlockSpec((None, ck, bn), lambda i, j: (layer, j, jnp.minimum(i, nb - 1)))]
    args = [a, w]
    if has_bias:
        in_specs.append(pl.BlockSpec((1, bn), lambda i, j: (0, col(i))))
        args.append(bias)
    rope_static = None
    if rope is not None:
        cos, sin, dk, k_scale = rope
        in_specs += [pl.BlockSpec((bm, dk // 2), lambda i, j: (row(i, j), 0))] * 2
        args += [cos, sin]
        rope_static = (n // RET_IN_PARTS // bn, dk, k_scale)
    if slab is None:
        out_specs = [pl.BlockSpec((bm, bn), lambda i, j: (row(i, j), col(i)))]
        out_shape = [jax.ShapeDtypeStruct((m, n), out_dtype)]
    else:
        out_specs = [pl.BlockSpec((bn // slab, bm, slab), lambda i, j: (col(i), row(i, j), 0))]
        out_shape = [jax.ShapeDtypeStruct((n // slab, m, slab), out_dtype)]
    scratch = [pltpu.VMEM((2, kd, bn), BF16)]
    if side is not None:
        c_col, ada_w, ada_b, first_col = side
        layers, da, na = ada_w.shape
        todo = layers * na - first_col
        sw = SIDE_COLS
        nside = todo // sw
        assert nside * sw == todo and first_col % sw == 0 and na % sw == 0 and nside <= nb * mb

        def sblk(i, j):
            return jnp.minimum(col(i) * mb + row(i, j), nside - 1)

        def flat(i, j):
            return first_col // sw + sblk(i, j)

        per_layer = na // sw
        in_specs += [pl.BlockSpec((da, 1), lambda i, j: (0, 0)),
                     pl.BlockSpec((None, da, sw), lambda i, j: (flat(i, j) // per_layer, 0, flat(i, j) % per_layer)),
                     pl.BlockSpec((None, 1, sw), lambda i, j: (flat(i, j) // per_layer, 0, flat(i, j) % per_layer))]
        args += [c_col, ada_w, ada_b.reshape(layers, 1, na)]
        out_specs.append(pl.BlockSpec((1, sw), lambda i, j: (0, sblk(i, j))))
        out_shape.append(jax.ShapeDtypeStruct((1, todo), F32))
        scratch.append(pltpu.VMEM((da, V7X_LANES), F32))
    out = pl.pallas_call(
        functools.partial(_mm1_kernel, nb=nb, ck=ck, act=act, has_bias=has_bias, rope=rope_static,
                          side=side is not None),
        grid=(nb + 1, mb),
        in_specs=in_specs,
        out_specs=out_specs,
        out_shape=out_shape,
        scratch_shapes=scratch,
        compiler_params=_params("arbitrary", "arbitrary"),
        name="matmul1",
    )(*args)
    return out if side is not None else out[0]


def _ret_kernel(lg_ref, q_ref, k_ref, v_ref, g_ref, gn_ref,
                o_ref, yacc_ref, st_ref, *, chunk, nsub):
    h, p, i = pl.program_id(0), pl.program_id(1), pl.program_id(2)
    nblk = pl.num_programs(2)
    blk_rows = chunk * nsub
    dk = q_ref.shape[1]
    lgf, lgb = lg_ref[0, h], lg_ref[1, h]

    @pl.when(i == 0)
    def _():
        st_ref[...] = jnp.zeros_like(st_ref)

    r = lax.broadcasted_iota(jnp.int32, (chunk, 1), 0).astype(F32)

    def wide(ref, rows):
        return jnp.concatenate([ref[0, rows, :], ref[1, rows, :]], axis=-1)

    def decay(expo):
        return jnp.broadcast_to(jnp.exp(expo), (chunk, dk)).astype(BF16)

    def state_update(k, kdec, v, lg):
        upd = lax.dot_general(k * kdec, v, (((0,), (0,)), ((), ())), preferred_element_type=F32)
        st_ref[...] = st_ref[...] * jnp.exp(jnp.full((1, 1), chunk * lg, F32)) + upd

    @pl.when(p == 0)
    def _():
        ri = lax.broadcasted_iota(jnp.int32, (chunk, chunk), 0)
        ci = lax.broadcasted_iota(jnp.int32, (chunk, chunk), 1)
        diff = (ri - ci).astype(F32)
        dmat = (jnp.where(diff >= 0, jnp.exp(jnp.maximum(diff, 0.0) * lgf), 0.0)
                + jnp.where(diff <= 0, jnp.exp(jnp.maximum(-diff, 0.0) * lgb), 0.0))
        qdec = decay((r + 1.0) * lgf)
        kdec = decay((chunk - 1.0 - r) * lgf)
        cdec = jnp.exp(jnp.full((1, 1), chunk * lgf, F32))
        for s in range(nsub):
            rows = pl.ds(s * chunk, chunk)
            q, k = q_ref[rows, :], k_ref[rows, :]
            sc = lax.dot_general(q, k, (((1,), (1,)), ((), ())), preferred_element_type=F32)
            scm = (sc * dmat).astype(BF16)
            qd, kd = q * qdec, k * kdec
            out_rows = pl.ds(pl.multiple_of(i * blk_rows + s * chunk, chunk), chunk)
            for hf in range(2):
                cols = slice(hf * dk, (hf + 1) * dk)
                v = v_ref[hf, rows, :]
                intra = jnp.dot(scm, v, preferred_element_type=F32)
                cross = jnp.dot(qd, st_ref[:, cols].astype(BF16), preferred_element_type=F32)
                yacc_ref[out_rows, cols] = intra + cross
                upd = lax.dot_general(kd, v, (((0,), (0,)), ((), ())), preferred_element_type=F32)
                st_ref[:, cols] = st_ref[:, cols] * cdec + upd

    @pl.when(p == 1)
    def _():
        qdec = decay((chunk - r) * lgb)
        kdec = decay(r * lgb)
        base = (nblk - 1 - i) * blk_rows
        for s in reversed(range(nsub)):
            rows = pl.ds(s * chunk, chunk)
            q, k, v = q_ref[rows, :], k_ref[rows, :], wide(v_ref, rows)
            cross = jnp.dot(q * qdec, st_ref[...].astype(BF16), preferred_element_type=F32)
            y = yacc_ref[pl.ds(pl.multiple_of(base + s * chunk, chunk), chunk), :] + cross
            mu = jnp.mean(y, axis=-1, keepdims=True)
            yc = y - mu
            var = jnp.mean(yc * yc, axis=-1, keepdims=True)
            yn = yc * lax.rsqrt(var + EPS) * gn_ref[...]
            g = wide(g_ref, rows).astype(F32)
            o_ref[rows, :] = (g * jax.nn.sigmoid(g) * yn).astype(o_ref.dtype)
            state_update(k, kdec, v, lgb)


def _retention(proj, lg, gn_g, heads):
    nslab, s, dk = proj.shape
    d = nslab * dk // 6
    dv = 2 * dk
    chunk = min(RET_CHUNK, s)
    blk = min(RET_BLOCK, s)
    nblk = s // blk

    def cidx(p, i):
        return i + p * (nblk - 1 - 2 * i)

    grid_spec = pltpu.PrefetchScalarGridSpec(
        num_scalar_prefetch=1,
        grid=(heads, 2, nblk),
        in_specs=[
            pl.BlockSpec((None, blk, dk), lambda h, p, i, lg: (h, cidx(p, i), 0)),
            pl.BlockSpec((None, blk, dk), lambda h, p, i, lg: (heads + h, cidx(p, i), 0)),
            pl.BlockSpec((2, blk, dk), lambda h, p, i, lg: (heads + h, cidx(p, i), 0)),
            pl.BlockSpec((2, blk, dk), lambda h, p, i, lg: (2 * heads + h, nblk - 1 - i * p, 0)),
            pl.BlockSpec((1, dv), lambda h, p, i, lg: (0, h)),
        ],
        out_specs=pl.BlockSpec((blk, dv), lambda h, p, i, lg: (nblk - 1 - i * p, h)),
        scratch_shapes=[pltpu.VMEM((s, dv), F32), pltpu.VMEM((dk, dv), F32)],
    )
    return pl.pallas_call(
        functools.partial(_ret_kernel, chunk=chunk, nsub=blk // chunk),
        grid_spec=grid_spec,
        out_shape=jax.ShapeDtypeStruct((s, 2 * d), BF16),
        compiler_params=_params("arbitrary", "arbitrary", "arbitrary"),
        name="retention",
    )(lg, proj, proj, proj, proj, gn_g)


def _dft_tables(n):
    j = np.arange(n)
    ang = 2.0 * np.pi * ((j[:, None] * j[None, :]) % n) / n
    return np.cos(ang), np.sin(ang)


def _fourier_stage1_kernel(h_ref, cs_ref, ma_ref, mb_ref, wr_ref, wi_ref, o_ref, *, tile):
    n1, jb, dg = h_ref.shape
    r = jnp.dot(h_ref[...].reshape(n1 * jb, dg), cs_ref[...], preferred_element_type=F32)
    r = r.astype(BF16).reshape(n1, jb, 2 * dg)
    half = n1 * tile
    for s in range(jb // tile):
        rs = r[:, s * tile:(s + 1) * tile, :].reshape(half, 2 * dg)
        t = (jnp.dot(ma_ref[...], rs[:, :dg], preferred_element_type=F32)
             + jnp.dot(mb_ref[...], rs[:, dg:], preferred_element_type=F32))
        tr, ti = t[:half], t[half:]
        wr, wi = wr_ref[s], wi_ref[s]
        rows = pl.ds(s * tile, tile)
        o_ref[0, :, rows, :] = (tr * wr - ti * wi).astype(o_ref.dtype).reshape(n1, tile, dg)
        o_ref[1, :, rows, :] = (tr * wi + ti * wr).astype(o_ref.dtype).reshape(n1, tile, dg)


def _fourier_stage2_kernel(t_ref, c_ref, s_ref, o_ref):
    n1 = t_ref.shape[1]
    res = [jnp.dot(c_ref[...], t_ref[0, kk], preferred_element_type=F32)
           + jnp.dot(s_ref[...], t_ref[1, kk], preferred_element_type=F32) for kk in range(n1)]
    o_ref[...] = jnp.swapaxes(jnp.stack(res), 0, 1).astype(o_ref.dtype)


def _fourier_real(h, groups):
    s, d = h.shape
    dg = d // groups
    n1 = DFT_N1
    n2 = s // n1
    tile = BF16_SUBLANE_TILE
    jb = min(DFT_STEP_ROWS, n2)

    cd, sd = _dft_tables(dg)
    cs = jnp.asarray(np.concatenate([cd, sd], axis=1), F32).astype(BF16)
    c1, s1 = _dft_tables(n1)
    eye = np.eye(tile)
    ma = jnp.asarray(np.kron(np.concatenate([c1, -s1], axis=0), eye), F32).astype(BF16)
    mb = jnp.asarray(np.kron(np.concatenate([-s1, -c1], axis=0), eye), F32).astype(BF16)
    tw = 2.0 * np.pi * (np.arange(n1)[None, :, None]
                        * (np.arange(n2 // tile)[:, None, None] * tile + np.arange(tile)[None, None, :])) / s
    tw = tw.reshape(n2 // tile, n1 * tile, 1)
    wr = jnp.asarray(np.cos(tw), F32)
    wi = jnp.asarray(-np.sin(tw), F32)
    tp = pl.pallas_call(
        functools.partial(_fourier_stage1_kernel, tile=tile),
        grid=(n2 // jb, groups),
        in_specs=[pl.BlockSpec((n1, jb, dg), lambda j, g: (0, j, g)),
                  pl.BlockSpec((dg, 2 * dg), lambda j, g: (0, 0)),
                  pl.BlockSpec((2 * n1 * tile, n1 * tile), lambda j, g: (0, 0)),
                  pl.BlockSpec((2 * n1 * tile, n1 * tile), lambda j, g: (0, 0)),
                  pl.BlockSpec((jb // tile, n1 * tile, 1), lambda j, g: (j, 0, 0)),
                  pl.BlockSpec((jb // tile, n1 * tile, 1), lambda j, g: (j, 0, 0))],
        out_specs=pl.BlockSpec((2, n1, jb, dg), lambda j, g: (0, 0, j, g)),
        out_shape=jax.ShapeDtypeStruct((2, n1, n2, d), BF16),
        compiler_params=_params("parallel", "parallel"),
        name="fourier_stage1",
    )(h.reshape(n1, n2, d), cs, ma, mb, wr, wi)

    c2, s2 = _dft_tables(n2)
    db = min(DFT_STEP_COLS, d)
    mixed = pl.pallas_call(
        _fourier_stage2_kernel,
        grid=(d // db,),
        in_specs=[pl.BlockSpec((2, n1, n2, db), lambda e: (0, 0, 0, e)),
                  pl.BlockSpec((n2, n2), lambda e: (0, 0)),
                  pl.BlockSpec((n2, n2), lambda e: (0, 0))],
        out_specs=pl.BlockSpec((n2, n1, db), lambda e: (0, 0, e)),
        out_shape=jax.ShapeDtypeStruct((n2, n1, d), BF16),
        compiler_params=_params("parallel"),
        name="fourier_stage2",
    )(tp, jnp.asarray(c2, F32).astype(BF16), jnp.asarray(s2, F32).astype(BF16))
    return mixed.reshape(s, d)


def kernel(x, c, ada_w, ada_b, norm_g, ret_w_in, ret_w_out, ret_gn_g, ret_decay_fwd,
           ret_decay_bwd, fno_w, fno_b, mlp_w1, mlp_w2):
    batch, s, d = x.shape
    assert batch == 1
    depth = ada_w.shape[0]
    heads, groups = RET_HEADS, FNO_GROUPS
    x = x.reshape(s, d)

    assert depth == 2 and ada_w.shape[2] == N_MOD * d
    c_col = c.reshape(d, 1)
    mods = [_adaln(c_col, ada_w, ada_b, 2 * d)[0], None]

    def mod(layer, idx):
        return mods[layer][:, idx * d:(idx + 1) * d]

    def gain(layer, idx):
        return norm_g[layer, idx].reshape(1, d)

    half = d // heads // 2
    inv = ROPE_BASE ** (-jnp.arange(half, dtype=F32) / half)
    ang = jnp.arange(s, dtype=F32)[:, None] * inv[None, :]
    cos, sin = jnp.cos(ang), jnp.sin(ang)

    (h,) = _row_call(x, gx=gain(0, 0), sc=mod(0, 1), sh=mod(0, 0))
    for layer in range(depth):
        occ = layer // 2
        if layer % 2 == 0:
            dk = d // heads
            proj, rest = _matmul1(h, ret_w_in, occ, rope=(cos, sin, dk, float(dk) ** -0.5),
                                  slab=dk, side=(c_col, ada_w, ada_b, 2 * d))
            mods = [jnp.concatenate([mods[0], rest[:, :4 * d]], axis=1), rest[:, 4 * d:]]
            lg = jnp.stack([jax.nn.log_sigmoid(ret_decay_fwd[occ].astype(F32)),
                            jax.nn.log_sigmoid(ret_decay_bwd[occ].astype(F32))])
            yh = _retention(proj, lg, ret_gn_g[occ].reshape(1, 2 * d), heads)
            y = _matmulk(yh, ret_w_out, occ)
        else:
            mixed = _fourier_real(h, groups)
            y = _matmul1(mixed, fno_w, occ, bias=fno_b[occ].reshape(1, d))
        x, h = _row_call(x, y=y, gate=mod(layer, 2), gy=gain(layer, 1),
                         gx=gain(layer, 2), sc=mod(layer, 4), sh=mod(layer, 3))
        a = _matmul1(h, mlp_w1, layer, act="relu2")
        y = _matmulk(a, mlp_w2, layer)
        if layer + 1 < depth:
            x, h = _row_call(x, y=y, gate=mod(layer, 5), gy=gain(layer, 3),
                             gx=gain(layer + 1, 0), sc=mod(layer + 1, 1), sh=mod(layer + 1, 0))
        else:
            (x,) = _row_call(x, y=y, gate=mod(layer, 5), gy=gain(layer, 3))
    return x.reshape(batch, s, d)
```

```python
import functools

import numpy as np
import jax
import jax.numpy as jnp
from jax import lax
from jax.experimental import pallas as pl
from jax.experimental.pallas import tpu as pltpu

N_MOD = 6
RET_IN_PARTS = 6
RET_HEADS = 16
FNO_GROUPS = 8
ROPE_BASE = 10000.0
EPS = 1e-6

V7X_VMEM_BYTES = 64 * 1024 * 1024
V7X_LANES = 128
BF16_SUBLANE_TILE = 16
VMEM_LIMIT_BYTES = V7X_VMEM_BYTES - 4 * 1024 * 1024

RET_CHUNK = 256
RET_BLOCK = 4096
DFT_N1 = 16
DFT_STEP_ROWS = 256
DFT_STEP_COLS = 256
ROW_BLOCK = 512
ADALN_COLS = 512
MM1_BLOCK = (1024, 1024)
MMK_BLOCK = (2048, 1024, 2048)
MM1_ROW_SUB = 512
MMK_ROW_SUB = 512
SIDE_COLS = 256

F32 = jnp.float32
BF16 = jnp.bfloat16


def _params(*sem):
    return pltpu.CompilerParams(dimension_semantics=sem, vmem_limit_bytes=VMEM_LIMIT_BYTES)


def _first_norm_kernel(c_ref, w_ref, b_ref, x_ref, gx_ref, h_ref, mod_ref, *, na):
    t = pl.program_id(0)

    @pl.when(t < na)
    def _():
        c = c_ref[...]
        s = c * jax.nn.sigmoid(c)
        mod_ref[t] = jnp.sum(w_ref[...] * s, axis=0, keepdims=True) + b_ref[...]

    @pl.when(t >= na)
    def _():
        sh = jnp.concatenate([mod_ref[j] for j in range(na // 2)], axis=-1)
        sc = jnp.concatenate([mod_ref[j] for j in range(na // 2, na)], axis=-1)
        h_ref[...] = (_rms(x_ref[...], gx_ref[...]) * (1.0 + sc) + sh).astype(BF16)


def _first_norm(x, c_col, ada_w, ada_b, gx):
    s, d = x.shape
    depth, _, n = ada_w.shape
    bn, bm = min(ADALN_COLS, d), min(ROW_BLOCK, s)
    na = 2 * d // bn
    return pl.pallas_call(
        functools.partial(_first_norm_kernel, na=na),
        grid=(na + s // bm,),
        in_specs=[pl.BlockSpec((d, 1), lambda t: (0, 0)),
                  pl.BlockSpec((None, d, bn), lambda t: (0, 0, jnp.minimum(t, na - 1))),
                  pl.BlockSpec((None, 1, bn), lambda t: (0, 0, jnp.minimum(t, na - 1))),
                  pl.BlockSpec((bm, d), lambda t: (jnp.maximum(t - na, 0), 0)),
                  pl.BlockSpec((1, d), lambda t: (0, 0))],
        out_specs=pl.BlockSpec((bm, d), lambda t: (jnp.maximum(t - na, 0), 0)),
        out_shape=jax.ShapeDtypeStruct((s, d), BF16),
        scratch_shapes=[pltpu.VMEM((na, 1, bn), F32)],
        compiler_params=_params("arbitrary"),
        name="first_norm",
    )(c_col, ada_w, ada_b.reshape(depth, 1, n), x, gx)


def _rms(x, g):
    ms = jnp.mean(x * x, axis=-1, keepdims=True)
    return x * lax.rsqrt(ms + EPS) * g


def _row_kernel(*refs, has_y, has_h):
    refs = list(refs)
    x_ref = refs.pop(0)
    x = x_ref[...]
    if has_y:
        y_ref, gate_ref, gy_ref = refs.pop(0), refs.pop(0), refs.pop(0)
    if has_h:
        gx_ref, sc_ref, sh_ref = refs.pop(0), refs.pop(0), refs.pop(0)
    if has_y:
        xo_ref = refs.pop(0)
        x = x + gate_ref[...] * _rms(y_ref[...].astype(F32), gy_ref[...])
        xo_ref[...] = x
    if has_h:
        h_ref = refs.pop(0)
        h_ref[...] = (_rms(x, gx_ref[...]) * (1.0 + sc_ref[...]) + sh_ref[...]).astype(BF16)


def _row_call(x, y=None, gate=None, gy=None, gx=None, sc=None, sh=None):
    s, d = x.shape
    bm = min(ROW_BLOCK, s)
    has_y, has_h = y is not None, gx is not None
    row = pl.BlockSpec((bm, d), lambda i: (i, 0))
    vec = pl.BlockSpec((1, d), lambda i: (0, 0))
    args, in_specs, out_specs, out_shape = [x], [row], [], []
    if has_y:
        args += [y, gate, gy]
        in_specs += [row, vec, vec]
        out_specs.append(row)
        out_shape.append(jax.ShapeDtypeStruct((s, d), F32))
    if has_h:
        args += [gx, sc, sh]
        in_specs += [vec, vec, vec]
        out_specs.append(row)
        out_shape.append(jax.ShapeDtypeStruct((s, d), BF16))
    out = pl.pallas_call(
        functools.partial(_row_kernel, has_y=has_y, has_h=has_h),
        grid=(s // bm,),
        in_specs=in_specs, out_specs=out_specs, out_shape=out_shape,
        compiler_params=_params("parallel"),
        name="row_norm",
    )(*args)
    return out


def _mmk_kernel(a_ref, w_ref, o_ref, wb_ref, acc_ref, *, nk, ck, bm):
    b, m = pl.program_id(0), pl.program_id(1)
    k = (b - 1) % nk
    rb = min(MMK_ROW_SUB, bm)

    def cast_next():
        wb_ref[b % 2, pl.ds(pl.multiple_of(m * ck, ck), ck), :] = w_ref[...].astype(BF16)

    def body(first, last):
        cast_next()
        wb = wb_ref.at[(b + 1) % 2]
        for r in range(bm // rb):
            part = jnp.dot(a_ref[r * rb:(r + 1) * rb, :], wb[...], preferred_element_type=F32)
            rows = pl.ds(pl.multiple_of(m * bm + r * rb, rb), rb)
            if not first:
                part = acc_ref[rows, :] + part
            if last:
                o_ref[r * rb:(r + 1) * rb, :] = part.astype(o_ref.dtype)
            else:
                acc_ref[rows, :] = part

    @pl.when(b == 0)
    def _():
        cast_next()

    @pl.when(jnp.logical_and(b > 0, k == 0))
    def _():
        body(True, False)

    @pl.when(jnp.logical_and(b > 0, jnp.logical_and(k > 0, k < nk - 1)))
    def _():
        body(False, False)

    @pl.when(jnp.logical_and(b > 0, k == nk - 1))
    def _():
        body(False, True)


def _matmulk(a, w, layer, halves=2):
    m, kd = a.shape
    _, _, n = w.shape
    bm, bn, bk = MMK_BLOCK
    nb, nk = n // bn, kd // bk
    mb = m // halves // bm
    ck = bk // mb
    nblocks = halves * nb * nk
    assert nk > 1 and mb * bm * halves == m and ck * mb == bk

    def dec(x):
        return x // (nk * nb), (x // nk) % nb, x % nk

    def a_map(b, j):
        hh, _, kk = dec(jnp.maximum(b - 1, 0))
        return hh * mb + jnp.where(b > 0, j, 0), kk

    def w_map(b, j):
        _, nn, kk = dec(jnp.minimum(b, nblocks - 1))
        return layer, kk * mb + j, nn

    def o_map(b, j):
        hh, nn, kk = dec(jnp.maximum(b - 1, 0))
        return hh * mb + jnp.where(jnp.logical_and(b > 0, kk == nk - 1), j, 0), nn

    return pl.pallas_call(
        functools.partial(_mmk_kernel, nk=nk, ck=ck, bm=bm),
        grid=(nblocks + 1, mb),
        in_specs=[pl.BlockSpec((bm, bk), a_map),
                  pl.BlockSpec((None, ck, bn), w_map)],
        out_specs=pl.BlockSpec((bm, bn), o_map),
        out_shape=jax.ShapeDtypeStruct((m, n), BF16),
        scratch_shapes=[pltpu.VMEM((2, bk, bn), BF16), pltpu.VMEM((m // halves, bn), F32)],
        compiler_params=_params("arbitrary", "arbitrary"),
        name="matmulk",
    )(a, w)


def _mm1_kernel(*refs, nb, ck, act, has_bias, rope, side):
    refs = list(refs)
    a_ref, w_ref = refs.pop(0), refs.pop(0)
    b_ref = refs.pop(0) if has_bias else None
    cos_ref, sin_ref = (refs.pop(0), refs.pop(0)) if rope else (None, None)
    if side:
        c_ref, aw_ref, ab_ref = refs.pop(0), refs.pop(0), refs.pop(0)
        o_ref, mo_ref, wb_ref, sb_ref = refs
    else:
        o_ref, wb_ref = refs
    n, m = pl.program_id(0), pl.program_id(1)

    def cast_next():
        wb_ref[n % 2, pl.ds(pl.multiple_of(m * ck, ck), ck), :] = w_ref[...].astype(BF16)

    def epilogue(acc):
        if has_bias:
            acc = acc + b_ref[...]
        if act == "relu2":
            acc = jnp.maximum(acc, 0.0)
            acc = acc * acc
        return acc

    def compute(rotary):
        cast_next()
        if side:
            lanes = sb_ref.shape[1]
            mv = [jnp.sum(aw_ref[:, t * lanes:(t + 1) * lanes] * sb_ref[...], axis=0, keepdims=True)
                  for t in range(aw_ref.shape[1] // lanes)]
            mo_ref[...] = jnp.concatenate(mv, axis=-1) + ab_ref[...]
        bm = a_ref.shape[0]
        rb = min(MM1_ROW_SUB, bm)
        wb = wb_ref.at[(n + 1) % 2]
        if rotary:
            nq, dk, k_scale = rope
            scale = jnp.where(n - 1 >= nq, k_scale, 1.0)
        for r in range(bm // rb):
            rows = slice(r * rb, (r + 1) * rb)
            acc = epilogue(jnp.dot(a_ref[rows, :], wb[...], preferred_element_type=F32))
            if rotary:
                c, s = cos_ref[rows, :] * scale, sin_ref[rows, :] * scale
                half = dk // 2
                parts = []
                for hh in range(acc.shape[1] // dk):
                    x1 = acc[:, hh * dk:hh * dk + half]
                    x2 = acc[:, hh * dk + half:(hh + 1) * dk]
                    parts += [x1 * c - x2 * s, x1 * s + x2 * c]
                acc = jnp.concatenate(parts, axis=-1)
            if len(o_ref.shape) == 3:
                slab = o_ref.shape[2]
                for hh in range(o_ref.shape[0]):
                    o_ref[hh, rows, :] = acc[:, hh * slab:(hh + 1) * slab].astype(o_ref.dtype)
            else:
                o_ref[rows, :] = acc.astype(o_ref.dtype)

    @pl.when(n == 0)
    def _():
        cast_next()

    if side:
        @pl.when(jnp.logical_and(n == 0, m == 0))
        def _():
            c = c_ref[...]
            sb_ref[...] = jnp.broadcast_to(c * jax.nn.sigmoid(c), sb_ref.shape)

    if rope:
        nqk = 2 * rope[0]

        @pl.when(jnp.logical_and(n > 0, n - 1 < nqk))
        def _():
            compute(True)

        @pl.when(n - 1 >= nqk)
        def _():
            compute(False)
    else:
        @pl.when(n > 0)
        def _():
            compute(False)


def _matmul1(a, w, layer, bias=None, act=None, out_dtype=BF16, rope=None, slab=None, side=None):
    m, kd = a.shape
    _, _, n = w.shape
    bm, bn = min(MM1_BLOCK[0], m), min(MM1_BLOCK[1], n)
    nb, mb = n // bn, m // bm
    ck = kd // mb
    has_bias = bias is not None

    def row(i, j):
        return jnp.where(i > 0, j, 0)

    def col(i):
        return jnp.maximum(i - 1, 0)

    in_specs = [pl.BlockSpec((bm, kd), lambda i, j: (row(i, j), 0)),
                pl.BlockSpec((None, ck, bn), lambda i, j: (layer, j, jnp.minimum(i, nb - 1)))]
    args = [a, w]
    if has_bias:
        in_specs.append(pl.BlockSpec((1, bn), lambda i, j: (0, col(i))))
        args.append(bias)
    rope_static = None
    if rope is not None:
        cos, sin, dk, k_scale = rope
        in_specs += [pl.BlockSpec((bm, dk // 2), lambda i, j: (row(i, j), 0))] * 2
        args += [cos, sin]
        rope_static = (n // RET_IN_PARTS // bn, dk, k_scale)
    if slab is None:
        out_specs = [pl.BlockSpec((bm, bn), lambda i, j: (row(i, j), col(i)))]
        out_shape = [jax.ShapeDtypeStruct((m, n), out_dtype)]
    else:
        out_specs = [pl.BlockSpec((bn // slab, bm, slab), lambda i, j: (col(i), row(i, j), 0))]
        out_shape = [jax.ShapeDtypeStruct((n // slab, m, slab), out_dtype)]
    scratch = [pltpu.VMEM((2, kd, bn), BF16)]
    if side is not None:
        c_col, ada_w, ada_b, first_col = side
        layers, da, na = ada_w.shape
        todo = layers * na - first_col
        sw = SIDE_COLS
        nside = todo // sw
        assert nside * sw == todo and first_col % sw == 0 and na % sw == 0 and nside <= nb * mb

        def sblk(i, j):
            return jnp.minimum(col(i) * mb + row(i, j), nside - 1)

        def flat(i, j):
            return first_col // sw + sblk(i, j)

        per_layer = na // sw
        in_specs += [pl.BlockSpec((da, 1), lambda i, j: (0, 0)),
                     pl.BlockSpec((None, da, sw), lambda i, j: (flat(i, j) // per_layer, 0, flat(i, j) % per_layer)),
                     pl.BlockSpec((None, 1, sw), lambda i, j: (flat(i, j) // per_layer, 0, flat(i, j) % per_layer))]
        args += [c_col, ada_w, ada_b.reshape(layers, 1, na)]
        out_specs.append(pl.BlockSpec((1, sw), lambda i, j: (0, sblk(i, j))))
        out_shape.append(jax.ShapeDtypeStruct((1, todo), F32))
        scratch.append(pltpu.VMEM((da, V7X_LANES), F32))
    out = pl.pallas_call(
        functools.partial(_mm1_kernel, nb=nb, ck=ck, act=act, has_bias=has_bias, rope=rope_static,
                          side=side is not None),
        grid=(nb + 1, mb),
        in_specs=in_specs,
        out_specs=out_specs,
        out_shape=out_shape,
        scratch_shapes=scratch,
        compiler_params=_params("arbitrary", "arbitrary"),
        name="matmul1",
    )(*args)
    return out if side is not None else out[0]


def _ret_kernel(lg_ref, q_ref, k_ref, v_ref, g_ref, gn_ref,
                o_ref, yacc_ref, st_ref, *, chunk, nsub):
    h, p, i = pl.program_id(0), pl.program_id(1), pl.program_id(2)
    nblk = pl.num_programs(2)
    blk_rows = chunk * nsub
    dk = q_ref.shape[1]
    lgf, lgb = lg_ref[0, h], lg_ref[1, h]

    @pl.when(i == 0)
    def _():
        st_ref[...] = jnp.zeros_like(st_ref)

    r = lax.broadcasted_iota(jnp.int32, (chunk, 1), 0).astype(F32)

    def wide(ref, rows):
        return jnp.concatenate([ref[0, rows, :], ref[1, rows, :]], axis=-1)

    def decay(expo):
        return jnp.broadcast_to(jnp.exp(expo), (chunk, dk)).astype(BF16)

    def state_update(k, kdec, v, lg):
        upd = lax.dot_general(k * kdec, v, (((0,), (0,)), ((), ())), preferred_element_type=F32)
        st_ref[...] = st_ref[...] * jnp.exp(jnp.full((1, 1), chunk * lg, F32)) + upd

    @pl.when(p == 0)
    def _():
        ri = lax.broadcasted_iota(jnp.int32, (chunk, chunk), 0)
        ci = lax.broadcasted_iota(jnp.int32, (chunk, chunk), 1)
        diff = (ri - ci).astype(F32)
        dmat = (jnp.where(diff >= 0, jnp.exp(jnp.maximum(diff, 0.0) * lgf), 0.0)
                + jnp.where(diff <= 0, jnp.exp(jnp.maximum(-diff, 0.0) * lgb), 0.0))
        qdec = decay((r + 1.0) * lgf)
        kdec = decay((chunk - 1.0 - r) * lgf)
        cdec = jnp.exp(jnp.full((1, 1), chunk * lgf, F32))
        for s in range(nsub):
            rows = pl.ds(s * chunk, chunk)
            q, k = q_ref[rows, :], k_ref[rows, :]
            sc = lax.dot_general(q, k, (((1,), (1,)), ((), ())), preferred_element_type=F32)
            scm = (sc * dmat).astype(BF16)
            qd, kd = q * qdec, k * kdec
            out_rows = pl.ds(pl.multiple_of(i * blk_rows + s * chunk, chunk), chunk)
            for hf in range(2):
                cols = slice(hf * dk, (hf + 1) * dk)
                v = v_ref[hf, rows, :]
                intra = jnp.dot(scm, v, preferred_element_type=F32)
                cross = jnp.dot(qd, st_ref[:, cols].astype(BF16), preferred_element_type=F32)
                yacc_ref[out_rows, cols] = intra + cross
                upd = lax.dot_general(kd, v, (((0,), (0,)), ((), ())), preferred_element_type=F32)
                st_ref[:, cols] = st_ref[:, cols] * cdec + upd

    @pl.when(p == 1)
    def _():
        qdec = decay((chunk - r) * lgb)
        kdec = decay(r * lgb)
        base = (nblk - 1 - i) * blk_rows
        for s in reversed(range(nsub)):
            rows = pl.ds(s * chunk, chunk)
            q, k, v = q_ref[rows, :], k_ref[rows, :], wide(v_ref, rows)
            cross = jnp.dot(q * qdec, st_ref[...].astype(BF16), preferred_element_type=F32)
            y = yacc_ref[pl.ds(pl.multiple_of(base + s * chunk, chunk), chunk), :] + cross
            mu = jnp.mean(y, axis=-1, keepdims=True)
            yc = y - mu
            var = jnp.mean(yc * yc, axis=-1, keepdims=True)
            yn = yc * lax.rsqrt(var + EPS) * gn_ref[...]
            g = wide(g_ref, rows).astype(F32)
            o_ref[rows, :] = (g * jax.nn.sigmoid(g) * yn).astype(o_ref.dtype)
            state_update(k, kdec, v, lgb)


def _retention(proj, lg, gn_g, heads):
    nslab, s, dk = proj.shape
    d = nslab * dk // 6
    dv = 2 * dk
    chunk = min(RET_CHUNK, s)
    blk = min(RET_BLOCK, s)
    nblk = s // blk

    def cidx(p, i):
        return i + p * (nblk - 1 - 2 * i)

    grid_spec = pltpu.PrefetchScalarGridSpec(
        num_scalar_prefetch=1,
        grid=(heads, 2, nblk),
        in_specs=[
            pl.BlockSpec((None, blk, dk), lambda h, p, i, lg: (h, cidx(p, i), 0)),
            pl.BlockSpec((None, blk, dk), lambda h, p, i, lg: (heads + h, cidx(p, i), 0)),
            pl.BlockSpec((2, blk, dk), lambda h, p, i, lg: (heads + h, cidx(p, i), 0)),
            pl.BlockSpec((2, blk, dk), lambda h, p, i, lg: (2 * heads + h, nblk - 1 - i * p, 0)),
            pl.BlockSpec((1, dv), lambda h, p, i, lg: (0, h)),
        ],
        out_specs=pl.BlockSpec((blk, dv), lambda h, p, i, lg: (nblk - 1 - i * p, h)),
        scratch_shapes=[pltpu.VMEM((s, dv), F32), pltpu.VMEM((dk, dv), F32)],
    )
    return pl.pallas_call(
        functools.partial(_ret_kernel, chunk=chunk, nsub=blk // chunk),
        grid_spec=grid_spec,
        out_shape=jax.ShapeDtypeStruct((s, 2 * d), BF16),
        compiler_params=_params("arbitrary", "arbitrary", "arbitrary"),
        name="retention",
    )(lg, proj, proj, proj, proj, gn_g)


def _dft_tables(n):
    j = np.arange(n)
    ang = 2.0 * np.pi * ((j[:, None] * j[None, :]) % n) / n
    return np.cos(ang), np.sin(ang)


def _fourier_stage1_kernel(h_ref, cs_ref, ma_ref, mb_ref, wr_ref, wi_ref, o_ref, *, tile):
    n1, jb, dg = h_ref.shape
    r = jnp.dot(h_ref[...].reshape(n1 * jb, dg), cs_ref[...], preferred_element_type=F32)
    r = r.astype(BF16).reshape(n1, jb, 2 * dg)
    half = n1 * tile
    for s in range(jb // tile):
        rs = r[:, s * tile:(s + 1) * tile, :].reshape(half, 2 * dg)
        t = (jnp.dot(ma_ref[...], rs[:, :dg], preferred_element_type=F32)
             + jnp.dot(mb_ref[...], rs[:, dg:], preferred_element_type=F32))
        tr, ti = t[:half], t[half:]
        wr, wi = wr_ref[s], wi_ref[s]
        rows = pl.ds(s * tile, tile)
        o_ref[0, :, rows, :] = (tr * wr - ti * wi).astype(o_ref.dtype).reshape(n1, tile, dg)
        o_ref[1, :, rows, :] = (tr * wi + ti * wr).astype(o_ref.dtype).reshape(n1, tile, dg)


def _fourier_stage2_kernel(t_ref, c_ref, s_ref, o_ref):
    n1 = t_ref.shape[1]
    res = [jnp.dot(c_ref[...], t_ref[0, kk], preferred_element_type=F32)
           + jnp.dot(s_ref[...], t_ref[1, kk], preferred_element_type=F32) for kk in range(n1)]
    o_ref[...] = jnp.swapaxes(jnp.stack(res), 0, 1).astype(o_ref.dtype)


def _fourier_real(h, groups):
    s, d = h.shape
    dg = d // groups
    n1 = DFT_N1
    n2 = s // n1
    tile = BF16_SUBLANE_TILE
    jb = min(DFT_STEP_ROWS, n2)

    cd, sd = _dft_tables(dg)
    cs = jnp.asarray(np.concatenate([cd, sd], axis=1), F32).astype(BF16)
    c1, s1 = _dft_tables(n1)
    eye = np.eye(tile)
    ma = jnp.asarray(np.kron(np.concatenate([c1, -s1], axis=0), eye), F32).astype(BF16)
    mb = jnp.asarray(np.kron(np.concatenate([-s1, -c1], axis=0), eye), F32).astype(BF16)
    tw = 2.0 * np.pi * (np.arange(n1)[None, :, None]
                        * (np.arange(n2 // tile)[:, None, None] * tile + np.arange(tile)[None, None, :])) / s
    tw = tw.reshape(n2 // tile, n1 * tile, 1)
    wr = jnp.asarray(np.cos(tw), F32)
    wi = jnp.asarray(-np.sin(tw), F32)
    tp = pl.pallas_call(
        functools.partial(_fourier_stage1_kernel, tile=tile),
        grid=(n2 // jb, groups),
        in_specs=[pl.BlockSpec((n1, jb, dg), lambda j, g: (0, j, g)),
                  pl.BlockSpec((dg, 2 * dg), lambda j, g: (0, 0)),
                  pl.BlockSpec((2 * n1 * tile, n1 * tile), lambda j, g: (0, 0)),
                  pl.BlockSpec((2 * n1 * tile, n1 * tile), lambda j, g: (0, 0)),
                  pl.BlockSpec((jb // tile, n1 * tile, 1), lambda j, g: (j, 0, 0)),
                  pl.BlockSpec((jb // tile, n1 * tile, 1), lambda j, g: (j, 0, 0))],
        out_specs=pl.BlockSpec((2, n1, jb, dg), lambda j, g: (0, 0, j, g)),
        out_shape=jax.ShapeDtypeStruct((2, n1, n2, d), BF16),
        compiler_params=_params("parallel", "parallel"),
        name="fourier_stage1",
    )(h.reshape(n1, n2, d), cs, ma, mb, wr, wi)

    c2, s2 = _dft_tables(n2)
    db = min(DFT_STEP_COLS, d)
    mixed = pl.pallas_call(
        _fourier_stage2_kernel,
        grid=(d // db,),
        in_specs=[pl.BlockSpec((2, n1, n2, db), lambda e: (0, 0, 0, e)),
                  pl.BlockSpec((n2, n2), lambda e: (0, 0)),
                  pl.BlockSpec((n2, n2), lambda e: (0, 0))],
        out_specs=pl.BlockSpec((n2, n1, db), lambda e: (0, 0, e)),
        out_shape=jax.ShapeDtypeStruct((n2, n1, d), BF16),
        compiler_params=_params("parallel"),
        name="fourier_stage2",
    )(tp, jnp.asarray(c2, F32).astype(BF16), jnp.asarray(s2, F32).astype(BF16))
    return mixed.reshape(s, d)


def kernel(x, c, ada_w, ada_b, norm_g, ret_w_in, ret_w_out, ret_gn_g, ret_decay_fwd,
           ret_decay_bwd, fno_w, fno_b, mlp_w1, mlp_w2):
    batch, s, d = x.shape
    assert batch == 1
    depth = ada_w.shape[0]
    heads, groups = RET_HEADS, FNO_GROUPS
    x = x.reshape(s, d)

    assert depth == 2 and ada_w.shape[2] == N_MOD * d
    c_col = c.reshape(d, 1)
    mods = [None, None]

    def mod(layer, idx):
        idx = idx - 2 if layer == 0 else idx
        return mods[layer][:, idx * d:(idx + 1) * d]

    def gain(layer, idx):
        return norm_g[layer, idx].reshape(1, d)

    half = d // heads // 2
    inv = ROPE_BASE ** (-jnp.arange(half, dtype=F32) / half)
    ang = jnp.arange(s, dtype=F32)[:, None] * inv[None, :]
    cos, sin = jnp.cos(ang), jnp.sin(ang)

    h = _first_norm(x, c_col, ada_w, ada_b, gain(0, 0))
    for layer in range(depth):
        occ = layer // 2
        if layer % 2 == 0:
            dk = d // heads
            proj, rest = _matmul1(h, ret_w_in, occ, rope=(cos, sin, dk, float(dk) ** -0.5),
                                  slab=dk, side=(c_col, ada_w, ada_b, 2 * d))
            mods = [rest[:, :4 * d], rest[:, 4 * d:]]
            lg = jnp.stack([jax.nn.log_sigmoid(ret_decay_fwd[occ].astype(F32)),
                            jax.nn.log_sigmoid(ret_decay_bwd[occ].astype(F32))])
            yh = _retention(proj, lg, ret_gn_g[occ].reshape(1, 2 * d), heads)
            y = _matmulk(yh, ret_w_out, occ)
        else:
            mixed = _fourier_real(h, groups)
            y = _matmul1(mixed, fno_w, occ, bias=fno_b[occ].reshape(1, d))
        x, h = _row_call(x, y=y, gate=mod(layer, 2), gy=gain(layer, 1),
                         gx=gain(layer, 2), sc=mod(layer, 4), sh=mod(layer, 3))
        a = _matmul1(h, mlp_w1, layer, act="relu2")
        y = _matmulk(a, mlp_w2, layer)
        if layer + 1 < depth:
            x, h = _row_call(x, y=y, gate=mod(layer, 5), gy=gain(layer, 3),
                             gx=gain(layer + 1, 0), sc=mod(layer + 1, 1), sh=mod(layer + 1, 0))
        else:
            (x,) = _row_call(x, y=y, gate=mod(layer, 5), gy=gain(layer, 3))
    return x.reshape(batch, s, d)
```
